```python
import jax, jax.numpy as jnp
from jax import lax
import numpy as np

D_MODEL = 1024
BATCH = 8
SEQ = 8192
DEPTH = 1

HG_HEADS = 4
HG_DK = 128
HG_DV = 128
HG_WIDTH = HG_HEADS * HG_DK
HG_VWIDTH = HG_HEADS * HG_DV
HG_CHUNK = 32
ATT_HEADS = 8
ATT_KV_HEADS = 2
ATT_GROUP = ATT_HEADS // ATT_KV_HEADS
ATT_HD = 64
ATT_WIDTH = ATT_HEADS * ATT_HD
ATT_KV_WIDTH = ATT_KV_HEADS * ATT_HD
WINDOW = 128
D_FF = 2816
CONV_W = 3
EPS = 1e-6

HG_Q0 = 0
HG_F0 = HG_Q0 + HG_WIDTH
HG_I0 = HG_F0 + HG_WIDTH
HG_G0 = HG_I0 + HG_VWIDTH
AT_Q0 = HG_G0 + HG_VWIDTH
AT_K0 = AT_Q0 + ATT_WIDTH
AT_V0 = AT_K0 + ATT_KV_WIDTH
GATE_A0 = AT_V0 + ATT_KV_WIDTH
GATE_B0 = GATE_A0 + D_MODEL
IN_COLS = GATE_B0 + D_MODEL

kernel_name = "hgrn2_swa_sink_gated_hybrid"


def rms_norm(x, g):
    xf = x.astype(jnp.float32)
    y = xf * lax.rsqrt(jnp.mean(xf * xf, axis=-1, keepdims=True) + EPS)
    return y.astype(x.dtype) * g


def alibi_slopes(n_heads):
    return 2.0 ** (-8.0 * jnp.arange(1, n_heads + 1, dtype=jnp.float32) / n_heads)


def hgrn2_chunked(q, k, v, log_f):
    b_, s_, h_, dk = q.shape
    dv = v.shape[-1]
    n = s_ // HG_CHUNK
    q = q.reshape(b_, n, HG_CHUNK, h_, dk)
    k = k.reshape(b_, n, HG_CHUNK, h_, dk)
    v = v.reshape(b_, n, HG_CHUNK, h_, dv)
    cum = jnp.cumsum(log_f.reshape(b_, n, HG_CHUNK, h_, dk), axis=2)
    ref = cum[:, :, HG_CHUNK // 2 - 1:HG_CHUNK // 2]
    scores = jnp.einsum('bnthk,bnshk->bnhts', q * jnp.exp(cum - ref), k * jnp.exp(ref - cum))
    causal = jnp.tril(jnp.ones((HG_CHUNK, HG_CHUNK), dtype=bool))
    scores = jnp.where(causal, scores, 0.0)
    o_intra = jnp.einsum('bnhts,bnshv->bnthv', scores, v)
    cum_end = cum[:, :, -1]
    q_inter = q * jnp.exp(cum)
    k_inter = k * jnp.exp(cum_end[:, :, None] - cum)
    decay = jnp.exp(cum_end)

    def step(state, xs):
        qn, kn, vn, dn = xs
        o = jnp.einsum('bthk,bhkv->bthv', qn, state)
        state = dn[..., None] * state + jnp.einsum('bshk,bshv->bhkv', kn, vn)
        return state, o

    state0 = jnp.zeros((b_, h_, dk, dv), dtype=jnp.float32)
    xs = (jnp.moveaxis(q_inter, 1, 0), jnp.moveaxis(k_inter, 1, 0),
          jnp.moveaxis(v, 1, 0).astype(jnp.float32), jnp.moveaxis(decay, 1, 0))
    _, o_inter = lax.scan(step, state0, xs)
    o = o_intra + jnp.moveaxis(o_inter, 0, 1)
    return o.reshape(b_, s_, h_, dv)


def hgrn2_branch(q_pre, f_pre, i_pre, g_pre, lb, out_g):
    b_, s_, _ = q_pre.shape
    q = jax.nn.silu(q_pre.astype(jnp.float32)) * (HG_DK ** -0.5)
    f = lb + (1.0 - lb) * jax.nn.sigmoid(f_pre.astype(jnp.float32))
    k = 1.0 - f
    log_f = jnp.log(f)
    rs = lambda t, d: t.reshape(b_, s_, HG_HEADS, d)
    o = hgrn2_chunked(rs(q, HG_DK), rs(k, HG_DK), rs(i_pre, HG_DV), rs(log_f, HG_DK))
    o = rms_norm(o, out_g.astype(jnp.float32))
    g = rs(g_pre, HG_DV).astype(jnp.float32)
    o = o * jax.nn.silu(g)
    return o.reshape(b_, s_, HG_VWIDTH).astype(q_pre.dtype)


def swa_sink_attention(q, k, v, q_g, k_g, sinks):
    b_, s_, _ = q.shape
    n = s_ // WINDOW
    q = rms_norm(q.reshape(b_, n, WINDOW, ATT_KV_HEADS, ATT_GROUP, ATT_HD), q_g)
    k = rms_norm(k.reshape(b_, n, WINDOW, ATT_KV_HEADS, ATT_HD), k_g)
    v = v.reshape(b_, n, WINDOW, ATT_KV_HEADS, ATT_HD)
    prev = lambda t: jnp.pad(t, ((0, 0), (1, 0), (0, 0), (0, 0), (0, 0)))[:, :-1]
    k2 = jnp.concatenate([prev(k), k], axis=2)
    v2 = jnp.concatenate([prev(v), v], axis=2)
    scores = jnp.einsum('bnikgd,bnjkd->bkgnij', q, k2).astype(jnp.float32) * (ATT_HD ** -0.5)
    i_idx = jnp.arange(WINDOW)[:, None]
    j_idx = jnp.arange(2 * WINDOW)[None, :]
    delta = i_idx + WINDOW - j_idx
    band = (delta >= 0) & (delta < WINDOW)
    valid = band[None] & ((jnp.arange(n)[:, None, None] > 0) | (j_idx >= WINDOW)[None])
    slopes = alibi_slopes(ATT_HEADS).reshape(ATT_KV_HEADS, ATT_GROUP, 1, 1, 1)
    scores = jnp.where(valid, scores - slopes * delta.astype(jnp.float32), -jnp.inf)
    sink = sinks.astype(jnp.float32).reshape(ATT_KV_HEADS, ATT_GROUP, 1, 1, 1)
    m = jnp.maximum(jnp.max(scores, axis=-1, keepdims=True), sink)
    e = jnp.exp(scores - m)
    probs = e / (jnp.sum(e, axis=-1, keepdims=True) + jnp.exp(sink - m))
    out = jnp.einsum('bkgnij,bnjkd->bnikgd', probs.astype(v.dtype), v2)
    return out.reshape(b_, s_, ATT_WIDTH)


def conv_glu(h, w_up, conv_w, conv_b, w_down):
    u = h @ w_up
    gate, val = u[..., :D_FF], u[..., D_FF:]
    s_ = h.shape[1]
    gp = jnp.pad(gate, ((0, 0), (CONV_W - 1, 0), (0, 0)))
    conv = conv_b + sum(conv_w[tap] * gp[:, tap:tap + s_] for tap in range(CONV_W))
    return (jax.nn.gelu(conv, approximate=False) * val) @ w_down


def setup_inputs(seed: int = 0) -> dict:
    key = jax.random.key(seed)
    ks = jax.random.split(key, 16)
    nrm = lambda k, shape, fan_in: jax.random.normal(k, shape, jnp.float32) * fan_in ** -0.5
    return {
        "x": jax.random.normal(ks[0], (BATCH, SEQ, D_MODEL), jnp.float32),
        "norm1_g": 1.0 + 0.02 * jax.random.normal(ks[1], (DEPTH, D_MODEL), jnp.float32),
        "w_in": nrm(ks[2], (DEPTH, D_MODEL, IN_COLS), D_MODEL),
        "hgrn_lb_logits": 0.1 * jax.random.normal(ks[3], (DEPTH + 1, HG_WIDTH), jnp.float32),
        "hgrn_out_g": 1.0 + 0.02 * jax.random.normal(ks[4], (DEPTH, HG_DV), jnp.float32),
        "q_norm_g": 1.0 + 0.02 * jax.random.normal(ks[5], (DEPTH, ATT_HD), jnp.float32),
        "k_norm_g": 1.0 + 0.02 * jax.random.normal(ks[6], (DEPTH, ATT_HD), jnp.float32),
        "attn_sinks": 0.5 * jax.random.normal(ks[7], (DEPTH, ATT_HEADS), jnp.float32),
        "w_branch_a": nrm(ks[8], (DEPTH, HG_VWIDTH, D_MODEL), HG_VWIDTH),
        "w_branch_b": nrm(ks[9], (DEPTH, ATT_WIDTH, D_MODEL), ATT_WIDTH),
        "w_out": nrm(ks[10], (DEPTH, D_MODEL, D_MODEL), D_MODEL),
        "norm2_g": 1.0 + 0.02 * jax.random.normal(ks[11], (DEPTH, D_MODEL), jnp.float32),
        "w_up": nrm(ks[12], (DEPTH, D_MODEL, 2 * D_FF), D_MODEL),
        "conv_w": nrm(ks[13], (DEPTH, CONV_W, D_FF), CONV_W),
        "conv_b": 0.02 * jax.random.normal(ks[14], (DEPTH, D_FF), jnp.float32),
        "w_down": nrm(ks[15], (DEPTH, D_FF, D_MODEL), D_FF),
    }


def reference(x, norm1_g, w_in, hgrn_lb_logits, hgrn_out_g, q_norm_g, k_norm_g, attn_sinks,
              w_branch_a, w_branch_b, w_out, norm2_g, w_up, conv_w, conv_b, w_down):
    lb_table = jnp.cumsum(jax.nn.softmax(hgrn_lb_logits.astype(jnp.float32), axis=0), axis=0)
    for l in range(DEPTH):
        h = rms_norm(x, norm1_g[l])
        cols = h @ w_in[l]
        o_a = hgrn2_branch(cols[..., HG_Q0:HG_F0], cols[..., HG_F0:HG_I0],
                           cols[..., HG_I0:HG_G0], cols[..., HG_G0:AT_Q0],
                           lb_table[l], hgrn_out_g[l])
        o_b = swa_sink_attention(cols[..., AT_Q0:AT_K0], cols[..., AT_K0:AT_V0],
                                 cols[..., AT_V0:GATE_A0], q_norm_g[l], k_norm_g[l], attn_sinks[l])
        gate_a = jax.nn.sigmoid(cols[..., GATE_A0:GATE_B0])
        gate_b = jax.nn.sigmoid(cols[..., GATE_B0:IN_COLS])
        mixed = gate_a * (o_a @ w_branch_a[l]) + gate_b * (o_b @ w_branch_b[l])
        x = x + mixed @ w_out[l]
        h2 = rms_norm(x, norm2_g[l])
        x = x + conv_glu(h2, w_up[l], conv_w[l], conv_b[l], w_down[l])
    return x
```

```python
import functools

import numpy as np
import jax
import jax.numpy as jnp
from jax import lax
from jax.experimental import pallas as pl
from jax.experimental.pallas import tpu as pltpu

F32 = jnp.float32
BF16 = jnp.bfloat16

D_MODEL = 1024
HG_HEADS = 4
HG_DK = 128
HG_DV = 128
HG_WIDTH = HG_HEADS * HG_DK
HG_CHUNK = 32
ATT_HEADS = 8
ATT_KV_HEADS = 2
ATT_GROUP = ATT_HEADS // ATT_KV_HEADS
ATT_HD = 64
ATT_WIDTH = ATT_HEADS * ATT_HD
WINDOW = 128
D_FF = 2816
CONV_W = 3
EPS = 1e-6
NEG = -1e30

LANES = 128
HG_TILE = 128
FF_CHUNK = 256
VMEM_LIMIT = 56 * 1024 * 1024


def _dot(a, b):
    return jnp.dot(a, b, preferred_element_type=F32)


def _dot_nt(a, b):
    return lax.dot_general(a, b, (((1,), (1,)), ((), ())), preferred_element_type=F32)


def _dot_tn(a, b):
    return lax.dot_general(a, b, (((0,), (0,)), ((), ())), preferred_element_type=F32)


def _rms(x, g):
    ms = jnp.mean(x * x, axis=-1, keepdims=True)
    return (x * lax.rsqrt(ms + EPS)) * g


def _inproj_kernel(x_ref, g1_ref, w_ref, lbl_ref, qg_ref, kg_ref, bdq_ref, bdk_ref,
                   hq_ref, hk_ref, hlf_ref, hv_ref, hg_ref, aq_ref, ak_ref, av_ref, ga_ref, gb_ref):
    h = _rms(x_ref[...], g1_ref[...]).astype(BF16)

    def proj(c0, width):
        return _dot(h, w_ref[:, c0:c0 + width])

    lbl = lbl_ref[...]
    e = jnp.exp(lbl - jnp.max(lbl, axis=0, keepdims=True))
    lb = e[0:1] / jnp.sum(e, axis=0, keepdims=True)

    q_pre = proj(0, HG_WIDTH)
    hq_ref[...] = (q_pre * jax.nn.sigmoid(q_pre)) * (HG_DK ** -0.5)
    f = lb + (1.0 - lb) * jax.nn.sigmoid(proj(HG_WIDTH, HG_WIDTH))
    hk_ref[...] = 1.0 - f
    hlf_ref[...] = jnp.log(f)
    hv_ref[...] = proj(2 * HG_WIDTH, HG_WIDTH).astype(BF16)
    g_pre = proj(3 * HG_WIDTH, HG_WIDTH)
    hg_ref[...] = (g_pre * jax.nn.sigmoid(g_pre)).astype(BF16)

    a_q = proj(4 * HG_WIDTH, ATT_WIDTH)
    ms_q = _dot((a_q * a_q).astype(BF16), bdq_ref[...]) * (1.0 / ATT_HD)
    aq_ref[...] = (((a_q * lax.rsqrt(ms_q + EPS)) * qg_ref[...]) * (ATT_HD ** -0.5)).astype(BF16)
    c_kv = 4 * HG_WIDTH + ATT_WIDTH
    a_k = proj(c_kv, 2 * LANES)
    ms_k = _dot((a_k * a_k).astype(BF16), bdk_ref[...]) * (1.0 / ATT_HD)
    ak_ref[...] = ((a_k * lax.rsqrt(ms_k + EPS)) * kg_ref[...]).astype(BF16)
    av_ref[...] = proj(c_kv + 2 * LANES, 2 * LANES).astype(BF16)

    c_g = c_kv + 4 * LANES
    for part in range(2):
        w = D_MODEL // 2
        ga_ref[:, part * w:(part + 1) * w] = jax.nn.sigmoid(proj(c_g + part * w, w)).astype(BF16)
        gb_ref[:, part * w:(part + 1) * w] = jax.nn.sigmoid(proj(c_g + D_MODEL + part * w, w)).astype(BF16)


def _inproj(x2, g1, w_ext, lbl, qg, kg, bdq, bdk, tm):
    m = x2.shape[0]
    ncols = w_ext.shape[1]
    row = lambda w: pl.BlockSpec((tm, w), lambda i: (i, 0))
    full = lambda a: pl.BlockSpec(a.shape, lambda i: (0,) * a.ndim)
    outs = [
        (HG_WIDTH, F32), (HG_WIDTH, F32), (HG_WIDTH, F32), (HG_WIDTH, BF16), (HG_WIDTH, BF16),
        (ATT_WIDTH, BF16), (2 * LANES, BF16), (2 * LANES, BF16), (D_MODEL, BF16), (D_MODEL, BF16),
    ]
    return pl.pallas_call(
        _inproj_kernel,
        grid=(m // tm,),
        in_specs=[row(D_MODEL), full(g1), pl.BlockSpec((D_MODEL, ncols), lambda i: (0, 0)),
                  full(lbl), full(qg), full(kg), full(bdq), full(bdk)],
        out_specs=[row(w) for w, _ in outs],
        out_shape=[jax.ShapeDtypeStruct((m, w), dt) for w, dt in outs],
        compiler_params=pltpu.CompilerParams(dimension_semantics=("arbitrary",), vmem_limit_bytes=VMEM_LIMIT),
        name="inproj",
    )(x2, g1, w_ext, lbl, qg, kg, bdq, bdk)


def _cumsum_rows(x, rows):
    shift = 1
    while shift < x.shape[0]:
        x = x + jnp.where(rows >= shift, pltpu.roll(x, shift, axis=0), 0.0)
        shift *= 2
    return x


def _block_row(g, blk, idx):
    n = g.shape[0] // blk
    r = g.reshape(n, blk, g.shape[1])[:, idx:idx + 1, :]
    return jnp.broadcast_to(r, (n, blk, g.shape[1])).reshape(g.shape)


def _hgrn_kernel(q_ref, k_ref, lf_ref, v_ref, sg_ref, og_ref, o_ref, st_ref, *, nsub):
    t = HG_TILE

    @pl.when(pl.program_id(1) == 0)
    def _():
        st_ref[...] = jnp.zeros_like(st_ref)

    row = lax.broadcasted_iota(jnp.int32, (t, t), 0)
    col = lax.broadcasted_iota(jnp.int32, (t, t), 1)
    ci = row // HG_CHUNK
    cj = col // HG_CHUNK
    level = jnp.where((ci == cj) & (col <= row), 1,
                      jnp.where((ci == cj + 1) & (cj % 2 == 0), 2,
                                jnp.where((ci >= 2) & (cj < 2), 3, 0)))
    rows = lax.broadcasted_iota(jnp.int32, (t, LANES), 0)
    og = og_ref[...]

    def tile(i, carry):
        r0 = pl.multiple_of(i * t, t)
        for h in range(HG_HEADS):
            cs = slice(h * HG_DK, (h + 1) * HG_DK)
            q = q_ref[pl.ds(r0, t), cs]
            k = k_ref[pl.ds(r0, t), cs]
            v = v_ref[pl.ds(r0, t), cs]
            g = _cumsum_rows(lf_ref[pl.ds(r0, t), cs], rows)

            r1 = _block_row(g, HG_CHUNK, HG_CHUNK // 2 - 1)
            s1 = _dot_nt((q * jnp.exp(g - r1)).astype(BF16), (k * jnp.exp(r1 - g)).astype(BF16))
            r2 = _block_row(g, 2 * HG_CHUNK, HG_CHUNK - 1)
            s2 = _dot_nt((q * jnp.exp(jnp.minimum(g - r2, 0.0))).astype(BF16),
                         (k * jnp.exp(jnp.minimum(r2 - g, 0.0))).astype(BF16))
            r3 = _block_row(g, 4 * HG_CHUNK, 2 * HG_CHUNK - 1)
            s3 = _dot_nt((q * jnp.exp(jnp.minimum(g - r3, 0.0))).astype(BF16),
                         (k * jnp.exp(jnp.minimum(r3 - g, 0.0))).astype(BF16))
            scores = jnp.where(level == 1, s1, jnp.where(level == 2, s2, jnp.where(level == 3, s3, 0.0)))
            o = _dot(scores.astype(BF16), v)

            g_end = g[t - 1:t, :]
            st = st_ref[h]
            o = o + _dot_nt((q * jnp.exp(g)).astype(BF16), st.astype(BF16))
            k_in = (k * jnp.exp(g_end - g)).astype(BF16)
            st_ref[h] = st * jnp.exp(g_end) + _dot_tn(v, k_in)

            o = _rms(o, og) * sg_ref[pl.ds(r0, t), cs].astype(F32)
            o_ref[pl.ds(r0, t), cs] = o.astype(BF16)
        return carry

    lax.fori_loop(0, nsub, tile, 0)


def _hgrn(hq, hk, hlf, hv, hg, og, ts):
    b, s, _ = hq.shape
    blk = pl.BlockSpec((None, ts, HG_WIDTH), lambda bi, si: (bi, si, 0))
    return pl.pallas_call(
        functools.partial(_hgrn_kernel, nsub=ts // HG_TILE),
        grid=(b, s // ts),
        in_specs=[blk, blk, blk, blk, blk, pl.BlockSpec(og.shape, lambda bi, si: (0, 0))],
        out_specs=blk,
        out_shape=jax.ShapeDtypeStruct((b, s, HG_WIDTH), BF16),
        scratch_shapes=[pltpu.VMEM((HG_HEADS, HG_DV, HG_DK), F32)],
        compiler_params=pltpu.CompilerParams(dimension_semantics=("arbitrary", "arbitrary"),
                                             vmem_limit_bytes=VMEM_LIMIT),
        name="hgrn",
    )(hq, hk, hlf, hv, hg, og)


def _swa_kernel(sink_ref, q_ref, k_ref, v_ref, bias_ref, o_ref, *, nsub, ts):
    w = WINDOW
    lane = lax.broadcasted_iota(jnp.int32, (1, LANES), 1)
    lo_b = (lane < ATT_HD).astype(BF16)
    hi_b = (lane >= ATT_HD).astype(BF16)
    lo = lax.broadcasted_iota(jnp.int32, (w, LANES), 1) < ATT_HD
    colk = lax.broadcasted_iota(jnp.int32, (1, 2 * w), 1)
    upper = lax.broadcasted_iota(jnp.int32, (2 * w, 1), 0) < w
    s_idx = pl.program_id(1)

    def block(i, carry):
        r0 = pl.multiple_of(i * w, w)
        k0 = pl.multiple_of(s_idx * ts + i * w, w)
        pad = jnp.where((colk < w) & (k0 == 0), NEG, 0.0)
        for j in range(ATT_HEADS // 2):
            c = j // (ATT_GROUP // 2)
            qc = q_ref[pl.ds(r0, w), j * LANES:(j + 1) * LANES]
            qs = jnp.concatenate([qc * lo_b, qc * hi_b], axis=0)
            kk = k_ref[pl.ds(k0, 2 * w), c * LANES:(c + 1) * LANES]
            s = _dot_nt(qs, kk) + bias_ref[j] + pad
            sink = jnp.where(upper, sink_ref[2 * j], sink_ref[2 * j + 1])
            m = jnp.maximum(jnp.max(s, axis=-1, keepdims=True), sink)
            e = jnp.exp(s - m)
            den = jnp.sum(e, axis=-1, keepdims=True) + jnp.exp(sink - m)
            p = (e / den).astype(BF16)
            vv = v_ref[pl.ds(k0, 2 * w), c * LANES:(c + 1) * LANES]
            o2 = _dot(p, vv)
            o_ref[pl.ds(r0, w), j * LANES:(j + 1) * LANES] = jnp.where(lo, o2[:w], o2[w:]).astype(BF16)
        return carry

    lax.fori_loop(0, nsub, block, 0)


def _swa(sinks, aq, akp, avp, bias, ts):
    b, s, _ = aq.shape
    sp = akp.shape[1]
    qblk = pl.BlockSpec((None, ts, ATT_WIDTH), lambda bi, si, *_: (bi, si, 0))
    kvblk = pl.BlockSpec((None, sp, 2 * LANES), lambda bi, si, *_: (bi, 0, 0))
    return pl.pallas_call(
        functools.partial(_swa_kernel, nsub=ts // WINDOW, ts=ts),
        grid_spec=pltpu.PrefetchScalarGridSpec(
            num_scalar_prefetch=1,
            grid=(b, s // ts),
            in_specs=[qblk, kvblk, kvblk, pl.BlockSpec(bias.shape, lambda bi, si, *_: (0, 0, 0))],
            out_specs=qblk,
        ),
        out_shape=jax.ShapeDtypeStruct((b, s, ATT_WIDTH), BF16),
        compiler_params=pltpu.CompilerParams(dimension_semantics=("arbitrary", "arbitrary"),
                                             vmem_limit_bytes=VMEM_LIMIT),
        name="swa",
    )(sinks, aq, akp, avp, bias)


def _swa_bias():
    w = WINDOW
    i = np.arange(w)[:, None]
    j = np.arange(2 * w)[None, :]
    delta = (i + w - j).astype(np.float32)
    band = (delta >= 0) & (delta < w)
    slopes = 2.0 ** (-8.0 * np.arange(1, ATT_HEADS + 1, dtype=np.float32) / ATT_HEADS)
    per_head = np.where(band[None], -slopes[:, None, None] * delta[None], np.float32(NEG)).astype(np.float32)
    return jnp.asarray(per_head.reshape(ATT_HEADS // 2, 2 * w, 2 * w))


def _mix_kernel(x_ref, oa_ref, ob_ref, ga_ref, gb_ref, wa_ref, wb_ref, wo_ref, g2_ref, x1_ref, h2_ref):
    mixed = (ga_ref[...].astype(F32) * _dot(oa_ref[...], wa_ref[...])
             + gb_ref[...].astype(F32) * _dot(ob_ref[...], wb_ref[...]))
    x1 = x_ref[...] + _dot(mixed.astype(BF16), wo_ref[...])
    x1_ref[...] = x1
    h2_ref[...] = _rms(x1, g2_ref[...]).astype(BF16)


def _mix(x2, oa, ob, ga, gb, wa, wb, wo, g2, tm):
    m = x2.shape[0]
    row = lambda w: pl.BlockSpec((tm, w), lambda i: (i, 0))
    full = lambda a: pl.BlockSpec(a.shape, lambda i: (0,) * a.ndim)
    return pl.pallas_call(
        _mix_kernel,
        grid=(m // tm,),
        in_specs=[row(D_MODEL), row(HG_WIDTH), row(ATT_WIDTH), row(D_MODEL), row(D_MODEL),
                  full(wa), full(wb), full(wo), full(g2)],
        out_specs=[row(D_MODEL), row(D_MODEL)],
        out_shape=[jax.ShapeDtypeStruct((m, D_MODEL), F32), jax.ShapeDtypeStruct((m, D_MODEL), BF16)],
        compiler_params=pltpu.CompilerParams(dimension_semantics=("arbitrary",), vmem_limit_bytes=VMEM_LIMIT),
        name="mix",
    )(x2, oa, ob, ga, gb, wa, wb, wo, g2)


def _ffn_kernel(x1_ref, h2_ref, wg_ref, wv_ref, cw_ref, cb_ref, wd_ref, o_ref, gs_ref, carry_ref, act_ref, *, tm):
    nchunk = D_FF // FF_CHUNK
    pad = 8

    @pl.when(pl.program_id(1) == 0)
    def _():
        carry_ref[...] = jnp.zeros_like(carry_ref)

    h2 = h2_ref[...]
    for c in range(nchunk):
        cs = slice(c * FF_CHUNK, (c + 1) * FF_CHUNK)
        gate = _dot(h2, wg_ref[:, cs])
        val = _dot(h2, wv_ref[:, cs])
        gs_ref[0:pad, :] = carry_ref[c]
        gs_ref[pad:pad + tm, :] = gate
        carry_ref[c] = gate[tm - pad:tm, :]
        conv = cb_ref[:, cs] + cw_ref[2:3, cs] * gate
        conv = conv + cw_ref[1:2, cs] * gs_ref[pad - 1:pad - 1 + tm, :]
        conv = conv + cw_ref[0:1, cs] * gs_ref[pad - 2:pad - 2 + tm, :]
        gelu = (0.5 * conv) * (1.0 + lax.erf(conv * (0.5 ** 0.5)))
        act_ref[:, cs] = (gelu * val).astype(BF16)
    o_ref[...] = x1_ref[...] + _dot(act_ref[...], wd_ref[...])


def _ffn(x1, h2, wg, wv, cw, cb, wd, tm):
    b, s, _ = x1.shape
    row = pl.BlockSpec((None, tm, D_MODEL), lambda bi, si: (bi, si, 0))
    full = lambda a: pl.BlockSpec(a.shape, lambda bi, si: (0,) * a.ndim)
    return pl.pallas_call(
        functools.partial(_ffn_kernel, tm=tm),
        grid=(b, s // tm),
        in_specs=[row, row, full(wg), full(wv), full(cw), full(cb), full(wd)],
        out_specs=row,
        out_shape=jax.ShapeDtypeStruct((b, s, D_MODEL), F32),
        scratch_shapes=[pltpu.VMEM((tm + 8, FF_CHUNK), F32),
                        pltpu.VMEM((D_FF // FF_CHUNK, 8, FF_CHUNK), F32),
                        pltpu.VMEM((tm, D_FF), BF16)],
        compiler_params=pltpu.CompilerParams(dimension_semantics=("arbitrary", "arbitrary"),
                                             vmem_limit_bytes=VMEM_LIMIT),
        name="ffn",
    )(x1, h2, wg, wv, cw, cb, wd)


def _block_diag_ones(n, blk):
    idx = np.arange(n) // blk
    return jnp.asarray((idx[:, None] == idx[None, :]).astype(np.float32), dtype=BF16)


def kernel(x, norm1_g, w_in, hgrn_lb_logits, hgrn_out_g, q_norm_g, k_norm_g, attn_sinks,
           w_branch_a, w_branch_b, w_out, norm2_g, w_up, conv_w, conv_b, w_down):
    b, s, d = x.shape
    assert d == D_MODEL and norm1_g.shape[0] == 1 and s % 512 == 0
    m = b * s
    tm = 512
    x2 = x.reshape(m, d)

    w0 = w_in[0]
    c_k = 4 * HG_WIDTH + ATT_WIDTH
    c_v = c_k + ATT_KV_HEADS * ATT_HD
    c_g = c_v + ATT_KV_HEADS * ATT_HD
    dup = lambda wcols: jnp.concatenate(
        [wcols[:, hh * ATT_HD:(hh + 1) * ATT_HD] for hh in range(ATT_KV_HEADS) for _ in range(2)], axis=1)
    w_ext = jnp.concatenate([w0[:, :c_k], dup(w0[:, c_k:c_v]), dup(w0[:, c_v:c_g]), w0[:, c_g:]], axis=1).astype(BF16)

    qg = jnp.tile(q_norm_g[0], ATT_HEADS).reshape(1, ATT_WIDTH)
    kg = jnp.tile(k_norm_g[0], 2 * ATT_KV_HEADS).reshape(1, 2 * LANES)
    hq, hk, hlf, hv, hg, aq, ak, av, ga, gb = _inproj(
        x2, norm1_g[0].reshape(1, d), w_ext, hgrn_lb_logits, qg, kg,
        _block_diag_ones(ATT_WIDTH, ATT_HD), _block_diag_ones(2 * LANES, ATT_HD), tm)

    r3 = lambda a: a.reshape(b, s, a.shape[-1])
    o_a = _hgrn(r3(hq), r3(hk), r3(hlf), r3(hv), r3(hg), hgrn_out_g[0].reshape(1, HG_DV), 512)

    padf = lambda a: jnp.pad(r3(a), ((0, 0), (WINDOW, 0), (0, 0)))
    o_b = _swa(attn_sinks[0], r3(aq), padf(ak), padf(av), _swa_bias(), 512)

    x1, h2 = _mix(x2, o_a.reshape(m, HG_WIDTH), o_b.reshape(m, ATT_WIDTH), ga, gb,
                  w_branch_a[0].astype(BF16), w_branch_b[0].astype(BF16), w_out[0].astype(BF16),
                  norm2_g[0].reshape(1, d), tm)

    out = _ffn(r3(x1), r3(h2), w_up[0, :, :D_FF].astype(BF16), w_up[0, :, D_FF:].astype(BF16),
               conv_w[0], conv_b[0].reshape(1, D_FF), w_down[0].astype(BF16), tm)
    return out
```

```python
import functools

import numpy as np
import jax
import jax.numpy as jnp
from jax import lax
from jax.experimental import pallas as pl
from jax.experimental.pallas import tpu as pltpu

F32 = jnp.float32
BF16 = jnp.bfloat16

D_MODEL = 1024
HG_HEADS = 4
HG_DK = 128
HG_DV = 128
HG_WIDTH = HG_HEADS * HG_DK
HG_CHUNK = 32
ATT_HEADS = 8
ATT_KV_HEADS = 2
ATT_GROUP = ATT_HEADS // ATT_KV_HEADS
ATT_HD = 64
ATT_WIDTH = ATT_HEADS * ATT_HD
WINDOW = 128
D_FF = 2816
CONV_W = 3
EPS = 1e-6
NEG = -1e30
LOG2E = 1.4426950408889634

LANES = 128
HG_TILE = 128
FF_CHUNK = 256
VMEM_LIMIT = 56 * 1024 * 1024


def _dot(a, b):
    return jnp.dot(a, b, preferred_element_type=F32)


def _dot_nt(a, b):
    return lax.dot_general(a, b, (((1,), (1,)), ((), ())), preferred_element_type=F32)


def _dot_tn(a, b):
    return lax.dot_general(a, b, (((0,), (0,)), ((), ())), preferred_element_type=F32)


def _rms(x, g):
    ms = jnp.mean(x * x, axis=-1, keepdims=True)
    return (x * lax.rsqrt(ms + EPS)) * g


def _inproj_kernel(x_ref, g1_ref, w_ref, lbl_ref, qg_ref, kg_ref, bdq_ref, bdk_ref,
                   hq_ref, hk_ref, hlf_ref, hv_ref, hg_ref, aq_ref, ak_ref, av_ref, ga_ref, gb_ref):
    h = _rms(x_ref[...], g1_ref[...]).astype(BF16)

    def proj(c0, width):
        return _dot(h, w_ref[:, c0:c0 + width])

    lbl = lbl_ref[...]
    e = jnp.exp(lbl - jnp.max(lbl, axis=0, keepdims=True))
    lb = e[0:1] / jnp.sum(e, axis=0, keepdims=True)

    q_pre = proj(0, HG_WIDTH)
    hq_ref[...] = (q_pre * jax.nn.sigmoid(q_pre)) * (HG_DK ** -0.5)
    f = lb + (1.0 - lb) * jax.nn.sigmoid(proj(HG_WIDTH, HG_WIDTH))
    hk_ref[...] = 1.0 - f
    hlf_ref[...] = jnp.log2(f)
    hv_ref[...] = proj(2 * HG_WIDTH, HG_WIDTH).astype(BF16)
    g_pre = proj(3 * HG_WIDTH, HG_WIDTH)
    hg_ref[...] = (g_pre * jax.nn.sigmoid(g_pre)).astype(BF16)

    a_q = proj(4 * HG_WIDTH, ATT_WIDTH)
    ms_q = _dot((a_q * a_q).astype(BF16), bdq_ref[...]) * (1.0 / ATT_HD)
    aq_ref[...] = (((a_q * lax.rsqrt(ms_q + EPS)) * qg_ref[...]) * (LOG2E * ATT_HD ** -0.5)).astype(BF16)
    c_kv = 4 * HG_WIDTH + ATT_WIDTH
    a_k = proj(c_kv, 2 * LANES)
    ms_k = _dot((a_k * a_k).astype(BF16), bdk_ref[...]) * (1.0 / ATT_HD)
    ak_ref[...] = ((a_k * lax.rsqrt(ms_k + EPS)) * kg_ref[...]).astype(BF16)
    av_ref[...] = proj(c_kv + 2 * LANES, 2 * LANES).astype(BF16)

    c_g = c_kv + 4 * LANES
    for part in range(2):
        w = D_MODEL // 2
        ga_ref[:, part * w:(part + 1) * w] = jax.nn.sigmoid(proj(c_g + part * w, w)).astype(BF16)
        gb_ref[:, part * w:(part + 1) * w] = jax.nn.sigmoid(proj(c_g + D_MODEL + part * w, w)).astype(BF16)


def _inproj(x2, g1, w_ext, lbl, qg, kg, bdq, bdk, tm):
    m = x2.shape[0]
    ncols = w_ext.shape[1]
    row = lambda w: pl.BlockSpec((tm, w), lambda i: (i, 0))
    full = lambda a: pl.BlockSpec(a.shape, lambda i: (0,) * a.ndim)
    outs = [
        (HG_WIDTH, F32), (HG_WIDTH, F32), (HG_WIDTH, F32), (HG_WIDTH, BF16), (HG_WIDTH, BF16),
        (ATT_WIDTH, BF16), (2 * LANES, BF16), (2 * LANES, BF16), (D_MODEL, BF16), (D_MODEL, BF16),
    ]
    return pl.pallas_call(
        _inproj_kernel,
        grid=(m // tm,),
        in_specs=[row(D_MODEL), full(g1), pl.BlockSpec((D_MODEL, ncols), lambda i: (0, 0)),
                  full(lbl), full(qg), full(kg), full(bdq), full(bdk)],
        out_specs=[row(w) for w, _ in outs],
        out_shape=[jax.ShapeDtypeStruct((m, w), dt) for w, dt in outs],
        compiler_params=pltpu.CompilerParams(dimension_semantics=("arbitrary",), vmem_limit_bytes=VMEM_LIMIT),
        name="inproj",
    )(x2, g1, w_ext, lbl, qg, kg, bdq, bdk)


def _block_row(g, blk, idx):
    n = g.shape[0] // blk
    r = g.reshape(n, blk, g.shape[1])[:, idx:idx + 1, :]
    return jnp.broadcast_to(r, (n, blk, g.shape[1])).reshape(g.shape)


def _hgrn_kernel(q_ref, k_ref, lf_ref, v_ref, sg_ref, og_ref, tri_ref, lvl_ref, o_ref, st_ref, *, nsub):
    t = HG_TILE

    @pl.when(pl.program_id(1) == 0)
    def _():
        st_ref[...] = jnp.zeros_like(st_ref)

    og = og_ref[...]
    level = lvl_ref[...]

    for i in range(nsub):
        rs = slice(i * t, (i + 1) * t)
        lf = lf_ref[rs, :]
        lf_hi = lf.astype(BF16)
        lf_lo = (lf - lf_hi.astype(F32)).astype(BF16)
        g_all = _dot(tri_ref[...], jnp.concatenate([lf_hi, lf_lo], axis=0))
        for h in range(HG_HEADS):
            cs = slice(h * HG_DK, (h + 1) * HG_DK)
            q = q_ref[rs, cs]
            k = k_ref[rs, cs]
            v = v_ref[rs, cs]
            g = g_all[:, cs]

            r1 = _block_row(g, HG_CHUNK, HG_CHUNK // 2 - 1)
            s1 = _dot_nt((q * jnp.exp2(g - r1)).astype(BF16), (k * jnp.exp2(r1 - g)).astype(BF16))
            r2 = _block_row(g, 2 * HG_CHUNK, HG_CHUNK - 1)
            s2 = _dot_nt((q * jnp.exp2(g - r2)).astype(BF16), (k * jnp.exp2(r2 - g)).astype(BF16))
            r3 = _block_row(g, 4 * HG_CHUNK, 2 * HG_CHUNK - 1)
            s3 = _dot_nt((q * jnp.exp2(g - r3)).astype(BF16), (k * jnp.exp2(r3 - g)).astype(BF16))
            scores = jnp.where(level == 1, s1, jnp.where(level == 2, s2, jnp.where(level == 3, s3, 0.0)))
            o = _dot(scores.astype(BF16), v)

            g_end = g[t - 1:t, :]
            st = st_ref[h]
            o = o + _dot_nt((q * jnp.exp2(g)).astype(BF16), st.astype(BF16))
            k_in = (k * jnp.exp2(g_end - g)).astype(BF16)
            st_ref[h] = st * jnp.exp2(g_end) + _dot_tn(v, k_in)

            o = _rms(o, og) * sg_ref[rs, cs].astype(F32)
            o_ref[rs, cs] = o.astype(BF16)


def _hgrn_consts():
    t = HG_TILE
    row = np.arange(t)[:, None]
    col = np.arange(t)[None, :]
    tri = (col <= row).astype(np.float32)
    ci, cj = row // HG_CHUNK, col // HG_CHUNK
    level = np.where((ci == cj) & (col <= row), 1,
                     np.where((ci == cj + 1) & (cj % 2 == 0), 2, np.where((ci >= 2) & (cj < 2), 3, 0)))
    return jnp.asarray(np.concatenate([tri, tri], axis=1), dtype=BF16), jnp.asarray(level, dtype=jnp.int32)


def _hgrn(hq, hk, hlf, hv, hg, og, ts):
    b, s, _ = hq.shape
    tri, level = _hgrn_consts()
    blk = pl.BlockSpec((None, ts, HG_WIDTH), lambda bi, si: (bi, si, 0))
    full = lambda a: pl.BlockSpec(a.shape, lambda bi, si: (0,) * a.ndim)
    return pl.pallas_call(
        functools.partial(_hgrn_kernel, nsub=ts // HG_TILE),
        grid=(b, s // ts),
        in_specs=[blk, blk, blk, blk, blk, full(og), full(tri), full(level)],
        out_specs=blk,
        out_shape=jax.ShapeDtypeStruct((b, s, HG_WIDTH), BF16),
        scratch_shapes=[pltpu.VMEM((HG_HEADS, HG_DV, HG_DK), F32)],
        compiler_params=pltpu.CompilerParams(dimension_semantics=("arbitrary", "arbitrary"),
                                             vmem_limit_bytes=VMEM_LIMIT),
        name="hgrn",
    )(hq, hk, hlf, hv, hg, og, tri, level)


def _swa_kernel(sink_ref, q_ref, kc_ref, kp_ref, vc_ref, vp_ref, bias_ref, o_ref, *, nsub):
    w = WINDOW
    lane = lax.broadcasted_iota(jnp.int32, (1, LANES), 1)
    lo_b = (lane < ATT_HD).astype(BF16)
    hi_b = (lane >= ATT_HD).astype(BF16)
    lo = lax.broadcasted_iota(jnp.int32, (w, LANES), 1) < ATT_HD
    colk = lax.broadcasted_iota(jnp.int32, (1, 2 * w), 1)
    upper = lax.broadcasted_iota(jnp.int32, (2 * w, 1), 0) < w
    pad = jnp.where((colk < w) & (pl.program_id(1) == 0), NEG, 0.0)

    for i in range(nsub):
        for j in range(ATT_HEADS // 2):
            c = j // (ATT_GROUP // 2)
            cs = slice(c * LANES, (c + 1) * LANES)
            if i == 0:
                kk = jnp.concatenate([kp_ref[:, cs], kc_ref[0:w, cs]], axis=0)
                vv = jnp.concatenate([vp_ref[:, cs], vc_ref[0:w, cs]], axis=0)
            else:
                kk = kc_ref[(i - 1) * w:(i + 1) * w, cs]
                vv = vc_ref[(i - 1) * w:(i + 1) * w, cs]
            qc = q_ref[i * w:(i + 1) * w, j * LANES:(j + 1) * LANES]
            qs = jnp.concatenate([qc * lo_b, qc * hi_b], axis=0)
            s = _dot_nt(qs, kk) + bias_ref[j]
            if i == 0:
                s = s + pad
            sink = jnp.where(upper, sink_ref[2 * j] * LOG2E, sink_ref[2 * j + 1] * LOG2E)
            m = jnp.maximum(jnp.max(s, axis=-1, keepdims=True), sink)
            e = jnp.exp2(s - m)
            den = jnp.sum(e, axis=-1, keepdims=True) + jnp.exp2(sink - m)
            o2 = _dot(e.astype(BF16), vv) * (1.0 / den)
            o_ref[i * w:(i + 1) * w, j * LANES:(j + 1) * LANES] = jnp.where(lo, o2[:w], o2[w:]).astype(BF16)


def _swa(sinks, aq, ak, av, bias, ts):
    b, s, _ = aq.shape
    nprev = ts // WINDOW
    qblk = pl.BlockSpec((None, ts, ATT_WIDTH), lambda bi, si, *_: (bi, si, 0))
    cur = pl.BlockSpec((None, ts, 2 * LANES), lambda bi, si, *_: (bi, si, 0))
    prev = pl.BlockSpec((None, WINDOW, 2 * LANES), lambda bi, si, *_: (bi, jnp.maximum(si * nprev - 1, 0), 0))
    return pl.pallas_call(
        functools.partial(_swa_kernel, nsub=ts // WINDOW),
        grid_spec=pltpu.PrefetchScalarGridSpec(
            num_scalar_prefetch=1,
            grid=(b, s // ts),
            in_specs=[qblk, cur, prev, cur, prev, pl.BlockSpec(bias.shape, lambda bi, si, *_: (0, 0, 0))],
            out_specs=qblk,
        ),
        out_shape=jax.ShapeDtypeStruct((b, s, ATT_WIDTH), BF16),
        compiler_params=pltpu.CompilerParams(dimension_semantics=("arbitrary", "arbitrary"),
                                             vmem_limit_bytes=VMEM_LIMIT),
        name="swa",
    )(sinks, aq, ak, ak, av, av, bias)


def _swa_bias():
    w = WINDOW
    i = np.arange(w)[:, None]
    j = np.arange(2 * w)[None, :]
    delta = (i + w - j).astype(np.float32)
    band = (delta >= 0) & (delta < w)
    slopes = 2.0 ** (-8.0 * np.arange(1, ATT_HEADS + 1, dtype=np.float32) / ATT_HEADS)
    per_head = np.where(band[None], -slopes[:, None, None] * delta[None] * np.float32(LOG2E), np.float32(NEG))
    return jnp.asarray(per_head.astype(np.float32).reshape(ATT_HEADS // 2, 2 * w, 2 * w))


def _mix_kernel(x_ref, oa_ref, ob_ref, ga_ref, gb_ref, wa_ref, wb_ref, wo_ref, g2_ref, x1_ref, h2_ref):
    mixed = (ga_ref[...].astype(F32) * _dot(oa_ref[...], wa_ref[...])
             + gb_ref[...].astype(F32) * _dot(ob_ref[...], wb_ref[...]))
    x1 = x_ref[...] + _dot(mixed.astype(BF16), wo_ref[...])
    x1_ref[...] = x1
    h2_ref[...] = _rms(x1, g2_ref[...]).astype(BF16)


def _mix(x2, oa, ob, ga, gb, wa, wb, wo, g2, tm):
    m = x2.shape[0]
    row = lambda w: pl.BlockSpec((tm, w), lambda i: (i, 0))
    full = lambda a: pl.BlockSpec(a.shape, lambda i: (0,) * a.ndim)
    return pl.pallas_call(
        _mix_kernel,
        grid=(m // tm,),
        in_specs=[row(D_MODEL), row(HG_WIDTH), row(ATT_WIDTH), row(D_MODEL), row(D_MODEL),
                  full(wa), full(wb), full(wo), full(g2)],
        out_specs=[row(D_MODEL), row(D_MODEL)],
        out_shape=[jax.ShapeDtypeStruct((m, D_MODEL), F32), jax.ShapeDtypeStruct((m, D_MODEL), BF16)],
        compiler_params=pltpu.CompilerParams(dimension_semantics=("arbitrary",), vmem_limit_bytes=VMEM_LIMIT),
        name="mix",
    )(x2, oa, ob, ga, gb, wa, wb, wo, g2)


def _ffn_kernel(x1_ref, h2_ref, wg_ref, wv_ref, cw_ref, cb_ref, wd_ref, o_ref, gs_ref, carry_ref, act_ref, *, tm):
    nchunk = D_FF // FF_CHUNK
    pad = 8

    @pl.when(pl.program_id(1) == 0)
    def _():
        carry_ref[...] = jnp.zeros_like(carry_ref)

    h2 = h2_ref[...]
    for c in range(nchunk):
        cs = slice(c * FF_CHUNK, (c + 1) * FF_CHUNK)
        gate = _dot(h2, wg_ref[:, cs])
        val = _dot(h2, wv_ref[:, cs])
        gs_ref[0:pad, :] = carry_ref[c]
        gs_ref[pad:pad + tm, :] = gate
        carry_ref[c] = gate[tm - pad:tm, :]
        conv = cb_ref[:, cs] + cw_ref[2:3, cs] * gate
        conv = conv + cw_ref[1:2, cs] * gs_ref[pad - 1:pad - 1 + tm, :]
        conv = conv + cw_ref[0:1, cs] * gs_ref[pad - 2:pad - 2 + tm, :]
        gelu = (0.5 * conv) * (1.0 + lax.erf(conv * (0.5 ** 0.5)))
        act_ref[:, cs] = (gelu * val).astype(BF16)
    o_ref[...] = x1_ref[...] + _dot(act_ref[...], wd_ref[...])


def _ffn(x1, h2, wg, wv, cw, cb, wd, tm):
    b, s, _ = x1.shape
    row = pl.BlockSpec((None, tm, D_MODEL), lambda bi, si: (bi, si, 0))
    full = lambda a: pl.BlockSpec(a.shape, lambda bi, si: (0,) * a.ndim)
    return pl.pallas_call(
        functools.partial(_ffn_kernel, tm=tm),
        grid=(b, s // tm),
        in_specs=[row, row, full(wg), full(wv), full(cw), full(cb), full(wd)],
        out_specs=row,
        out_shape=jax.ShapeDtypeStruct((b, s, D_MODEL), F32),
        scratch_shapes=[pltpu.VMEM((tm + 8, FF_CHUNK), F32),
                        pltpu.VMEM((D_FF // FF_CHUNK, 8, FF_CHUNK), F32),
                        pltpu.VMEM((tm, D_FF), BF16)],
        compiler_params=pltpu.CompilerParams(dimension_semantics=("arbitrary", "arbitrary"),
                                             vmem_limit_bytes=VMEM_LIMIT),
        name="ffn",
    )(x1, h2, wg, wv, cw, cb, wd)


def _block_diag_ones(n, blk):
    idx = np.arange(n) // blk
    return jnp.asarray((idx[:, None] == idx[None, :]).astype(np.float32), dtype=BF16)


def kernel(x, norm1_g, w_in, hgrn_lb_logits, hgrn_out_g, q_norm_g, k_norm_g, attn_sinks,
           w_branch_a, w_branch_b, w_out, norm2_g, w_up, conv_w, conv_b, w_down):
    b, s, d = x.shape
    assert d == D_MODEL and norm1_g.shape[0] == 1 and s % 512 == 0
    m = b * s
    tm = 512
    x2 = x.reshape(m, d)

    w0 = w_in[0]
    c_k = 4 * HG_WIDTH + ATT_WIDTH
    c_v = c_k + ATT_KV_HEADS * ATT_HD
    c_g = c_v + ATT_KV_HEADS * ATT_HD
    dup = lambda wcols: jnp.concatenate(
        [wcols[:, hh * ATT_HD:(hh + 1) * ATT_HD] for hh in range(ATT_KV_HEADS) for _ in range(2)], axis=1)
    w_ext = jnp.concatenate([w0[:, :c_k], dup(w0[:, c_k:c_v]), dup(w0[:, c_v:c_g]), w0[:, c_g:]], axis=1).astype(BF16)

    qg = jnp.tile(q_norm_g[0], ATT_HEADS).reshape(1, ATT_WIDTH)
    kg = jnp.tile(k_norm_g[0], 2 * ATT_KV_HEADS).reshape(1, 2 * LANES)
    hq, hk, hlf, hv, hg, aq, ak, av, ga, gb = _inproj(
        x2, norm1_g[0].reshape(1, d), w_ext, hgrn_lb_logits, qg, kg,
        _block_diag_ones(ATT_WIDTH, ATT_HD), _block_diag_ones(2 * LANES, ATT_HD), tm)

    r3 = lambda a: a.reshape(b, s, a.shape[-1])
    o_a = _hgrn(r3(hq), r3(hk), r3(hlf), r3(hv), r3(hg), hgrn_out_g[0].reshape(1, HG_DV), 1024)
    o_b = _swa(attn_sinks[0], r3(aq), r3(ak), r3(av), _swa_bias(), 512)

    x1, h2 = _mix(x2, o_a.reshape(m, HG_WIDTH), o_b.reshape(m, ATT_WIDTH), ga, gb,
                  w_branch_a[0].astype(BF16), w_branch_b[0].astype(BF16), w_out[0].astype(BF16),
                  norm2_g[0].reshape(1, d), tm)

    out = _ffn(r3(x1), r3(h2), w_up[0, :, :D_FF].astype(BF16), w_up[0, :, D_FF:].astype(BF16),
               conv_w[0], conv_b[0].reshape(1, D_FF), w_down[0].astype(BF16), tm)
    return out
```

```python
import functools

import numpy as np
import jax
import jax.numpy as jnp
from jax import lax
from jax.experimental import pallas as pl
from jax.experimental.pallas import tpu as pltpu

F32 = jnp.float32
BF16 = jnp.bfloat16

D_MODEL = 1024
HG_HEADS = 4
HG_DK = 128
HG_DV = 128
HG_WIDTH = HG_HEADS * HG_DK
HG_CHUNK = 32
ATT_HEADS = 8
ATT_KV_HEADS = 2
ATT_GROUP = ATT_HEADS // ATT_KV_HEADS
ATT_HD = 64
ATT_WIDTH = ATT_HEADS * ATT_HD
WINDOW = 128
D_FF = 2816
CONV_W = 3
EPS = 1e-6
NEG = -1e30
LOG2E = 1.4426950408889634

LANES = 128
HG_TILE = 128
FF_CHUNK = 256
VMEM_LIMIT = 56 * 1024 * 1024


def _dot(a, b):
    return jnp.dot(a, b, preferred_element_type=F32)


def _dot_nt(a, b):
    return lax.dot_general(a, b, (((1,), (1,)), ((), ())), preferred_element_type=F32)


def _dot_tn(a, b):
    return lax.dot_general(a, b, (((0,), (0,)), ((), ())), preferred_element_type=F32)


def _rms(x, g):
    ms = jnp.mean(x * x, axis=-1, keepdims=True)
    return (x * lax.rsqrt(ms + EPS)) * g


def _inproj_kernel(x_ref, g1_ref, w_ref, lbl_ref, qg_ref, kg_ref, bdq_ref, bdk_ref,
                   hq_ref, hk_ref, hlf_ref, hv_ref, hg_ref, aq_ref, ak_ref, av_ref, ga_ref, gb_ref):
    h = _rms(x_ref[...], g1_ref[...]).astype(BF16)

    def proj(c0, width):
        return _dot(h, w_ref[:, c0:c0 + width])

    lbl = lbl_ref[...]
    e = jnp.exp(lbl - jnp.max(lbl, axis=0, keepdims=True))
    lb = e[0:1] / jnp.sum(e, axis=0, keepdims=True)

    q_pre = proj(0, HG_WIDTH)
    hq_ref[...] = (q_pre * jax.nn.sigmoid(q_pre)) * (HG_DK ** -0.5)
    f = lb + (1.0 - lb) * jax.nn.sigmoid(proj(HG_WIDTH, HG_WIDTH))
    hk_ref[...] = 1.0 - f
    hlf_ref[...] = jnp.log2(f)
    hv_ref[...] = proj(2 * HG_WIDTH, HG_WIDTH).astype(BF16)
    g_pre = proj(3 * HG_WIDTH, HG_WIDTH)
    hg_ref[...] = (g_pre * jax.nn.sigmoid(g_pre)).astype(BF16)

    a_q = proj(4 * HG_WIDTH, ATT_WIDTH)
    ms_q = _dot((a_q * a_q).astype(BF16), bdq_ref[...]) * (1.0 / ATT_HD)
    aq_ref[...] = (((a_q * lax.rsqrt(ms_q + EPS)) * qg_ref[...]) * (LOG2E * ATT_HD ** -0.5)).astype(BF16)
    c_kv = 4 * HG_WIDTH + ATT_WIDTH
    a_k = proj(c_kv, 2 * LANES)
    ms_k = _dot((a_k * a_k).astype(BF16), bdk_ref[...]) * (1.0 / ATT_HD)
    ak_ref[...] = ((a_k * lax.rsqrt(ms_k + EPS)) * kg_ref[...]).astype(BF16)
    av_ref[...] = proj(c_kv + 2 * LANES, 2 * LANES).astype(BF16)

    c_g = c_kv + 4 * LANES
    for part in range(2):
        w = D_MODEL // 2
        ga_ref[:, part * w:(part + 1) * w] = jax.nn.sigmoid(proj(c_g + part * w, w)).astype(BF16)
        gb_ref[:, part * w:(part + 1) * w] = jax.nn.sigmoid(proj(c_g + D_MODEL + part * w, w)).astype(BF16)


def _inproj(x2, g1, w_ext, lbl, qg, kg, bdq, bdk, tm):
    m = x2.shape[0]
    ncols = w_ext.shape[1]
    row = lambda w: pl.BlockSpec((tm, w), lambda i: (i, 0))
    full = lambda a: pl.BlockSpec(a.shape, lambda i: (0,) * a.ndim)
    outs = [
        (HG_WIDTH, F32), (HG_WIDTH, F32), (HG_WIDTH, F32), (HG_WIDTH, BF16), (HG_WIDTH, BF16),
        (ATT_WIDTH, BF16), (2 * LANES, BF16), (2 * LANES, BF16), (D_MODEL, BF16), (D_MODEL, BF16),
    ]
    return pl.pallas_call(
        _inproj_kernel,
        grid=(m // tm,),
        in_specs=[row(D_MODEL), full(g1), pl.BlockSpec((D_MODEL, ncols), lambda i: (0, 0)),
                  full(lbl), full(qg), full(kg), full(bdq), full(bdk)],
        out_specs=[row(w) for w, _ in outs],
        out_shape=[jax.ShapeDtypeStruct((m, w), dt) for w, dt in outs],
        compiler_params=pltpu.CompilerParams(dimension_semantics=("arbitrary",), vmem_limit_bytes=VMEM_LIMIT),
        name="inproj",
    )(x2, g1, w_ext, lbl, qg, kg, bdq, bdk)


def _block_row(g, blk, idx):
    n = g.shape[0] // blk
    r = g.reshape(n, blk, g.shape[1])[:, idx:idx + 1, :]
    return jnp.broadcast_to(r, (n, blk, g.shape[1])).reshape(g.shape)


def _hgrn_kernel(q_ref, k_ref, lf_ref, v_ref, sg_ref, og_ref, tri_ref, lvl_ref, o_ref, st_ref, *, nsub):
    t = HG_TILE

    @pl.when(pl.program_id(1) == 0)
    def _():
        st_ref[...] = jnp.zeros_like(st_ref)

    og = og_ref[...]
    level = lvl_ref[...]

    for i in range(nsub):
        rs = slice(i * t, (i + 1) * t)
        lf = lf_ref[rs, :]
        lf_hi = lf.astype(BF16)
        lf_lo = (lf - lf_hi.astype(F32)).astype(BF16)
        g_all = _dot(tri_ref[...], jnp.concatenate([lf_hi, lf_lo], axis=0))
        for h in range(HG_HEADS):
            cs = slice(h * HG_DK, (h + 1) * HG_DK)
            q = q_ref[rs, cs]
            k = k_ref[rs, cs]
            v = v_ref[rs, cs]
            g = g_all[:, cs]

            r1 = _block_row(g, HG_CHUNK, HG_CHUNK // 2 - 1)
            s1 = _dot_nt((q * jnp.exp2(g - r1)).astype(BF16), (k * jnp.exp2(r1 - g)).astype(BF16))
            r2 = _block_row(g, 2 * HG_CHUNK, HG_CHUNK - 1)
            s2 = _dot_nt((q * jnp.exp2(g - r2)).astype(BF16), (k * jnp.exp2(r2 - g)).astype(BF16))
            r3 = _block_row(g, 4 * HG_CHUNK, 2 * HG_CHUNK - 1)
            s3 = _dot_nt((q * jnp.exp2(g - r3)).astype(BF16), (k * jnp.exp2(r3 - g)).astype(BF16))
            scores = jnp.where(level == 1, s1, jnp.where(level == 2, s2, jnp.where(level == 3, s3, 0.0)))
            o = _dot(scores.astype(BF16), v)

            g_end = g[t - 1:t, :]
            st = st_ref[h]
            o = o + _dot_nt((q * jnp.exp2(g)).astype(BF16), st.astype(BF16))
            k_in = (k * jnp.exp2(g_end - g)).astype(BF16)
            st_ref[h] = st * jnp.exp2(g_end) + _dot_tn(v, k_in)

            o = _rms(o, og) * sg_ref[rs, cs].astype(F32)
            o_ref[rs, cs] = o.astype(BF16)


def _hgrn_consts():
    t = HG_TILE
    row = np.arange(t)[:, None]
    col = np.arange(t)[None, :]
    tri = (col <= row).astype(np.float32)
    ci, cj = row // HG_CHUNK, col // HG_CHUNK
    level = np.where((ci == cj) & (col <= row), 1,
                     np.where((ci == cj + 1) & (cj % 2 == 0), 2, np.where((ci >= 2) & (cj < 2), 3, 0)))
    return jnp.asarray(np.concatenate([tri, tri], axis=1), dtype=BF16), jnp.asarray(level, dtype=jnp.int32)


def _hgrn(hq, hk, hlf, hv, hg, og, ts):
    b, s, _ = hq.shape
    tri, level = _hgrn_consts()
    blk = pl.BlockSpec((None, ts, HG_WIDTH), lambda bi, si: (bi, si, 0))
    full = lambda a: pl.BlockSpec(a.shape, lambda bi, si: (0,) * a.ndim)
    return pl.pallas_call(
        functools.partial(_hgrn_kernel, nsub=ts // HG_TILE),
        grid=(b, s // ts),
        in_specs=[blk, blk, blk, blk, blk, full(og), full(tri), full(level)],
        out_specs=blk,
        out_shape=jax.ShapeDtypeStruct((b, s, HG_WIDTH), BF16),
        scratch_shapes=[pltpu.VMEM((HG_HEADS, HG_DV, HG_DK), F32)],
        compiler_params=pltpu.CompilerParams(dimension_semantics=("arbitrary", "arbitrary"),
                                             vmem_limit_bytes=VMEM_LIMIT),
        name="hgrn",
    )(hq, hk, hlf, hv, hg, og, tri, level)


def _swa_kernel(sink_ref, q_ref, kc_ref, kp_ref, vc_ref, vp_ref, bias_ref, o_ref, *, nsub):
    w = WINDOW
    lane = lax.broadcasted_iota(jnp.int32, (1, LANES), 1)
    lo_b = (lane < ATT_HD).astype(BF16)
    hi_b = (lane >= ATT_HD).astype(BF16)
    lo = lax.broadcasted_iota(jnp.int32, (w, LANES), 1) < ATT_HD
    colk = lax.broadcasted_iota(jnp.int32, (1, 2 * w), 1)
    upper = lax.broadcasted_iota(jnp.int32, (2 * w, 1), 0) < w
    pad = jnp.where((colk < w) & (pl.program_id(1) == 0), NEG, 0.0)

    for i in range(nsub):
        for j in range(ATT_HEADS // 2):
            c = j // (ATT_GROUP // 2)
            cs = slice(c * LANES, (c + 1) * LANES)
            if i == 0:
                kk = jnp.concatenate([kp_ref[:, cs], kc_ref[0:w, cs]], axis=0)
                vv = jnp.concatenate([vp_ref[:, cs], vc_ref[0:w, cs]], axis=0)
            else:
                kk = kc_ref[(i - 1) * w:(i + 1) * w, cs]
                vv = vc_ref[(i - 1) * w:(i + 1) * w, cs]
            qc = q_ref[i * w:(i + 1) * w, j * LANES:(j + 1) * LANES]
            qs = jnp.concatenate([qc * lo_b, qc * hi_b], axis=0)
            s = _dot_nt(qs, kk) + bias_ref[j]
            if i == 0:
                s = s + pad
            sink = jnp.where(upper, sink_ref[2 * j] * LOG2E, sink_ref[2 * j + 1] * LOG2E)
            m = jnp.maximum(jnp.max(s, axis=-1, keepdims=True), sink)
            e = jnp.exp2(s - m)
            den = jnp.sum(e, axis=-1, keepdims=True) + jnp.exp2(sink - m)
            o2 = _dot(e.astype(BF16), vv) * (1.0 / den)
            o_ref[i * w:(i + 1) * w, j * LANES:(j + 1) * LANES] = jnp.where(lo, o2[:w], o2[w:]).astype(BF16)


def _swa(sinks, aq, ak, av, bias, ts):
    b, s, _ = aq.shape
    nprev = ts // WINDOW
    qblk = pl.BlockSpec((None, ts, ATT_WIDTH), lambda bi, si, *_: (bi, si, 0))
    cur = pl.BlockSpec((None, ts, 2 * LANES), lambda bi, si, *_: (bi, si, 0))
    prev = pl.BlockSpec((None, WINDOW, 2 * LANES), lambda bi, si, *_: (bi, jnp.maximum(si * nprev - 1, 0), 0))
    return pl.pallas_call(
        functools.partial(_swa_kernel, nsub=ts // WINDOW),
        grid_spec=pltpu.PrefetchScalarGridSpec(
            num_scalar_prefetch=1,
            grid=(b, s // ts),
            in_specs=[qblk, cur, prev, cur, prev, pl.BlockSpec(bias.shape, lambda bi, si, *_: (0, 0, 0))],
            out_specs=qblk,
        ),
        out_shape=jax.ShapeDtypeStruct((b, s, ATT_WIDTH), BF16),
        compiler_params=pltpu.CompilerParams(dimension_semantics=("arbitrary", "arbitrary"),
                                             vmem_limit_bytes=VMEM_LIMIT),
        name="swa",
    )(sinks, aq, ak, ak, av, av, bias)


def _swa_bias():
    w = WINDOW
    i = np.arange(w)[:, None]
    j = np.arange(2 * w)[None, :]
    delta = (i + w - j).astype(np.float32)
    band = (delta >= 0) & (delta < w)
    slopes = 2.0 ** (-8.0 * np.arange(1, ATT_HEADS + 1, dtype=np.float32) / ATT_HEADS)
    per_head = np.where(band[None], -slopes[:, None, None] * delta[None] * np.float32(LOG2E), np.float32(NEG))
    return jnp.asarray(per_head.astype(np.float32).reshape(ATT_HEADS // 2, 2 * w, 2 * w))


def _post_kernel(x_ref, oa_ref, ob_ref, ga_ref, gb_ref, wa_ref, wb_ref, wo_ref, g2_ref,
                 wg_ref, wv_ref, cw_ref, cb_ref, wd_ref, o_ref, x1_ref, h2_ref, gs_ref, carry_ref, act_ref, *, tm):
    nchunk = D_FF // FF_CHUNK
    pad = 8

    @pl.when(pl.program_id(1) == 0)
    def _():
        carry_ref[...] = jnp.zeros_like(carry_ref)

    mixed = (ga_ref[...].astype(F32) * _dot(oa_ref[...], wa_ref[...])
             + gb_ref[...].astype(F32) * _dot(ob_ref[...], wb_ref[...]))
    x1_ref[...] = x_ref[...] + _dot(mixed.astype(BF16), wo_ref[...])
    h2_ref[...] = _rms(x1_ref[...], g2_ref[...]).astype(BF16)

    h2 = h2_ref[...]
    for c in range(nchunk):
        cs = slice(c * FF_CHUNK, (c + 1) * FF_CHUNK)
        gate = _dot(h2, wg_ref[:, cs])
        val = _dot(h2, wv_ref[:, cs])
        gs_ref[0:pad, :] = carry_ref[c]
        gs_ref[pad:pad + tm, :] = gate
        carry_ref[c] = gate[tm - pad:tm, :]
        conv = cb_ref[:, cs] + cw_ref[2:3, cs] * gate
        conv = conv + cw_ref[1:2, cs] * gs_ref[pad - 1:pad - 1 + tm, :]
        conv = conv + cw_ref[0:1, cs] * gs_ref[pad - 2:pad - 2 + tm, :]
        gelu = (0.5 * conv) * (1.0 + lax.erf(conv * (0.5 ** 0.5)))
        act_ref[:, cs] = (gelu * val).astype(BF16)
    o_ref[...] = x1_ref[...] + _dot(act_ref[...], wd_ref[...])


def _post(x, oa, ob, ga, gb, wa, wb, wo, g2, wg, wv, cw, cb, wd, tm):
    b, s, _ = x.shape
    row = lambda w: pl.BlockSpec((None, tm, w), lambda bi, si: (bi, si, 0))
    full = lambda a: pl.BlockSpec(a.shape, lambda bi, si: (0,) * a.ndim, pipeline_mode=pl.Buffered(1))
    return pl.pallas_call(
        functools.partial(_post_kernel, tm=tm),
        grid=(b, s // tm),
        in_specs=[row(D_MODEL), row(HG_WIDTH), row(ATT_WIDTH), row(D_MODEL), row(D_MODEL),
                  full(wa), full(wb), full(wo), full(g2), full(wg), full(wv), full(cw), full(cb), full(wd)],
        out_specs=row(D_MODEL),
        out_shape=jax.ShapeDtypeStruct((b, s, D_MODEL), F32),
        scratch_shapes=[pltpu.VMEM((tm, D_MODEL), F32),
                        pltpu.VMEM((tm, D_MODEL), BF16),
                        pltpu.VMEM((tm + 8, FF_CHUNK), F32),
                        pltpu.VMEM((D_FF // FF_CHUNK, 8, FF_CHUNK), F32),
                        pltpu.VMEM((tm, D_FF), BF16)],
        compiler_params=pltpu.CompilerParams(dimension_semantics=("arbitrary", "arbitrary"),
                                             vmem_limit_bytes=VMEM_LIMIT),
        name="post",
    )(x, oa, ob, ga, gb, wa, wb, wo, g2, wg, wv, cw, cb, wd)


def _block_diag_ones(n, blk):
    idx = np.arange(n) // blk
    return jnp.asarray((idx[:, None] == idx[None, :]).astype(np.float32), dtype=BF16)


def kernel(x, norm1_g, w_in, hgrn_lb_logits, hgrn_out_g, q_norm_g, k_norm_g, attn_sinks,
           w_branch_a, w_branch_b, w_out, norm2_g, w_up, conv_w, conv_b, w_down):
    b, s, d = x.shape
    assert d == D_MODEL and norm1_g.shape[0] == 1 and s % 512 == 0
    m = b * s
    tm = 512
    x2 = x.reshape(m, d)

    w0 = w_in[0]
    c_k = 4 * HG_WIDTH + ATT_WIDTH
    c_v = c_k + ATT_KV_HEADS * ATT_HD
    c_g = c_v + ATT_KV_HEADS * ATT_HD
    dup = lambda wcols: jnp.concatenate(
        [wcols[:, hh * ATT_HD:(hh + 1) * ATT_HD] for hh in range(ATT_KV_HEADS) for _ in range(2)], axis=1)
    w_ext = jnp.concatenate([w0[:, :c_k], dup(w0[:, c_k:c_v]), dup(w0[:, c_v:c_g]), w0[:, c_g:]], axis=1).astype(BF16)

    qg = jnp.tile(q_norm_g[0], ATT_HEADS).reshape(1, ATT_WIDTH)
    kg = jnp.tile(k_norm_g[0], 2 * ATT_KV_HEADS).reshape(1, 2 * LANES)
    hq, hk, hlf, hv, hg, aq, ak, av, ga, gb = _inproj(
        x2, norm1_g[0].reshape(1, d), w_ext, hgrn_lb_logits, qg, kg,
        _block_diag_ones(ATT_WIDTH, ATT_HD), _block_diag_ones(2 * LANES, ATT_HD), tm)

    r3 = lambda a: a.reshape(b, s, a.shape[-1])
    o_a = _hgrn(r3(hq), r3(hk), r3(hlf), r3(hv), r3(hg), hgrn_out_g[0].reshape(1, HG_DV), 1024)
    o_b = _swa(attn_sinks[0], r3(aq), r3(ak), r3(av), _swa_bias(), 512)

    return _post(x, o_a, o_b, r3(ga), r3(gb),
                 w_branch_a[0].astype(BF16), w_branch_b[0].astype(BF16), w_out[0].astype(BF16),
                 norm2_g[0].reshape(1, d), w_up[0, :, :D_FF].astype(BF16), w_up[0, :, D_FF:].astype(BF16),
                 conv_w[0], conv_b[0].reshape(1, D_FF), w_down[0].astype(BF16), tm)
```

```python
import functools

import numpy as np
import jax
import jax.numpy as jnp
from jax import lax
from jax.experimental import pallas as pl
from jax.experimental.pallas import tpu as pltpu

F32 = jnp.float32
BF16 = jnp.bfloat16

D_MODEL = 1024
HG_HEADS = 4
HG_DK = 128
HG_DV = 128
HG_WIDTH = HG_HEADS * HG_DK
HG_CHUNK = 32
ATT_HEADS = 8
ATT_KV_HEADS = 2
ATT_GROUP = ATT_HEADS // ATT_KV_HEADS
ATT_HD = 64
ATT_WIDTH = ATT_HEADS * ATT_HD
ATT_KV_WIDTH = ATT_KV_HEADS * ATT_HD
WINDOW = 128
D_FF = 2816
CONV_W = 3
EPS = 1e-6
NEG = -1e30
LOG2E = 1.4426950408889634

LANES = 128
BF16_ROWS = 16
HG_TILE = 128
FF_CHUNK = 256
VMEM_LIMIT = 56 * 1024 * 1024

C_HQ, C_HF, C_HI, C_HG = 0, HG_WIDTH, 2 * HG_WIDTH, 3 * HG_WIDTH
C_AQ = 4 * HG_WIDTH
C_AKV = C_AQ + ATT_WIDTH
C_GA = C_AKV + 2 * ATT_KV_WIDTH
C_GB = C_GA + D_MODEL
IN_COLS = C_GB + D_MODEL


def _dot(a, b):
    return jnp.dot(a, b, preferred_element_type=F32)


def _dot_nt(a, b):
    return lax.dot_general(a, b, (((1,), (1,)), ((), ())), preferred_element_type=F32)


def _dot_tn(a, b):
    return lax.dot_general(a, b, (((0,), (0,)), ((), ())), preferred_element_type=F32)


def _rms(x, g):
    ms = jnp.mean(x * x, axis=-1, keepdims=True)
    return (x * lax.rsqrt(ms + EPS)) * g


def _inproj_kernel(x_ref, g1_ref, w_ref, lbl_ref, og_ref, qg_ref, kg_ref, bdq_ref, bdk_ref,
                   hq_ref, hk_ref, hlf_ref, hv_ref, hg_ref, aq_ref, ak_ref, av_ref, ga_ref, gb_ref, wb_ref):
    @pl.when(pl.program_id(0) == 0)
    def _():
        wb_ref[...] = w_ref[...].astype(BF16)

    h = _rms(x_ref[...], g1_ref[...]).astype(BF16)

    def proj(c0, width):
        return _dot(h, wb_ref[:, c0:c0 + width])

    lbl = lbl_ref[...]
    e = jnp.exp(lbl - jnp.max(lbl, axis=0, keepdims=True))
    lb = e[0:1] / jnp.sum(e, axis=0, keepdims=True)

    q_pre = proj(C_HQ, HG_WIDTH)
    hq_ref[...] = (q_pre * jax.nn.sigmoid(q_pre)) * (HG_DK ** -0.5)
    f = lb + (1.0 - lb) * jax.nn.sigmoid(proj(C_HF, HG_WIDTH))
    hk_ref[...] = 1.0 - f
    hlf_ref[...] = jnp.log2(f)
    hv_ref[...] = proj(C_HI, HG_WIDTH).astype(BF16)
    g_pre = proj(C_HG, HG_WIDTH)
    hg_ref[...] = ((g_pre * jax.nn.sigmoid(g_pre)) * og_ref[...]).astype(BF16)

    a_q = proj(C_AQ, ATT_WIDTH)
    ms_q = _dot((a_q * a_q).astype(BF16), bdq_ref[...]) * (1.0 / ATT_HD)
    aq_ref[...] = (((a_q * lax.rsqrt(ms_q + EPS)) * qg_ref[...]) * (LOG2E * ATT_HD ** -0.5)).astype(BF16)

    kv = proj(C_AKV, 2 * ATT_KV_WIDTH)
    a_k = kv[:, :ATT_KV_WIDTH]
    ms_k = _dot((a_k * a_k).astype(BF16), bdk_ref[...]) * (1.0 / ATT_HD)
    lo = lax.broadcasted_iota(jnp.int32, (1, LANES), 1) < ATT_HD
    for src, dst in (((a_k * lax.rsqrt(ms_k + EPS)) * kg_ref[...], ak_ref), (kv[:, ATT_KV_WIDTH:], av_ref)):
        swapped = pltpu.roll(src, ATT_HD, axis=1)
        dst[:, :LANES] = jnp.where(lo, src, swapped).astype(BF16)
        dst[:, LANES:] = jnp.where(lo, swapped, src).astype(BF16)

    for part in range(2):
        w = D_MODEL // 2
        ga_ref[:, part * w:(part + 1) * w] = jax.nn.sigmoid(proj(C_GA + part * w, w)).astype(BF16)
        gb_ref[:, part * w:(part + 1) * w] = jax.nn.sigmoid(proj(C_GB + part * w, w)).astype(BF16)


def _inproj(x2, g1, w, lbl, og, qg, kg, bdq, bdk, tm):
    m = x2.shape[0]
    row = lambda c: pl.BlockSpec((tm, c), lambda i: (i, 0))
    full = lambda a: pl.BlockSpec(a.shape, lambda i: (0,) * a.ndim)
    outs = [
        (HG_WIDTH, F32), (HG_WIDTH, F32), (HG_WIDTH, F32), (HG_WIDTH, BF16), (HG_WIDTH, BF16),
        (ATT_WIDTH, BF16), (2 * LANES, BF16), (2 * LANES, BF16), (D_MODEL, BF16), (D_MODEL, BF16),
    ]
    return pl.pallas_call(
        _inproj_kernel,
        grid=(m // tm,),
        in_specs=[row(D_MODEL), full(g1),
                  pl.BlockSpec(w.shape, lambda i: (0, 0), pipeline_mode=pl.Buffered(1)),
                  full(lbl), full(og), full(qg), full(kg), full(bdq), full(bdk)],
        out_specs=[row(c) for c, _ in outs],
        out_shape=[jax.ShapeDtypeStruct((m, c), dt) for c, dt in outs],
        scratch_shapes=[pltpu.VMEM(w.shape, BF16)],
        compiler_params=pltpu.CompilerParams(dimension_semantics=("arbitrary",), vmem_limit_bytes=VMEM_LIMIT),
        name="inproj",
    )(x2, g1, w, lbl, og, qg, kg, bdq, bdk)


def _hgrn_head(q, k, v, g, sg, st, level):
    c = HG_CHUNK
    n = HG_TILE // c
    qs = [q[i * c:(i + 1) * c] for i in range(n)]
    ks = [k[i * c:(i + 1) * c] for i in range(n)]
    gs = [g[i * c:(i + 1) * c] for i in range(n)]
    mid = [gc[c // 2 - 1:c // 2] for gc in gs]
    end = [gc[c - 1:c] for gc in gs]
    p2 = jnp.exp2
    cat = lambda parts: jnp.concatenate([p.astype(BF16) for p in parts], axis=0)
    zero = jnp.zeros((c, HG_DK), BF16)

    s1 = _dot_nt(cat([qs[i] * p2(gs[i] - mid[i]) for i in range(n)]),
                 cat([ks[i] * p2(mid[i] - gs[i]) for i in range(n)]))
    k2_0 = ks[0] * p2(end[0] - gs[0])
    k2_2 = ks[2] * p2(end[2] - gs[2])
    q2_1 = qs[1] * p2(gs[1] - end[0])
    q2_3 = qs[3] * p2(gs[3] - end[2])
    s2 = _dot_nt(cat([zero, q2_1, zero, q2_3]), cat([k2_0, zero, k2_2, zero]))
    k3_0 = k2_0 * p2(end[1] - end[0])
    k3_1 = ks[1] * p2(end[1] - gs[1])
    q3_2 = qs[2] * p2(gs[2] - end[1])
    q3_3 = q2_3 * p2(end[2] - end[1])
    s3 = _dot_nt(cat([zero, zero, q3_2, q3_3]), cat([k3_0, k3_1, zero, zero]))
    scores = jnp.where(level == 1, s1, jnp.where(level == 2, s2, s3))
    o = _dot(scores.astype(BF16), v)

    d_half = p2(end[3] - end[1])
    k_in = cat([k3_0 * d_half, k3_1 * d_half, k2_2 * p2(end[3] - end[2]), ks[3] * p2(end[3] - gs[3])])
    q_in = cat([qs[0] * p2(gs[0]), q2_1 * p2(end[0]), q3_2 * p2(end[1]), q3_3 * p2(end[1])])
    o = o + _dot_nt(q_in, st.astype(BF16))
    st_new = st * p2(end[3]) + _dot_tn(v, k_in)

    ms = jnp.mean(o * o, axis=-1, keepdims=True)
    return (o * lax.rsqrt(ms + EPS)) * sg, st_new


def _hgrn_tile(i, q_ref, k_ref, lf_ref, v_ref, sg_ref, tri_ref, level, o_ref, st_ref):
    t = HG_TILE
    rs = slice(i * t, (i + 1) * t)
    lf = lf_ref[rs, :]
    lf_hi = lf.astype(BF16)
    lf_lo = (lf - lf_hi.astype(F32)).astype(BF16)
    g_all = _dot(tri_ref[...], jnp.concatenate([lf_hi, lf_lo], axis=0))
    for h in range(HG_HEADS):
        cs = slice(h * HG_DK, (h + 1) * HG_DK)
        o, st_new = _hgrn_head(q_ref[rs, cs], k_ref[rs, cs], v_ref[rs, cs], g_all[:, cs],
                               sg_ref[rs, cs].astype(F32), st_ref[h], level)
        st_ref[h] = st_new
        o_ref[rs, cs] = o.astype(BF16)


def _swa_block(i, first_tile, sink_ref, q_ref, kc_ref, kp_ref, vc_ref, vp_ref, bias_ref, o_ref):
    w = WINDOW
    lane = lax.broadcasted_iota(jnp.int32, (1, LANES), 1)
    lo_b = (lane < ATT_HD).astype(BF16)
    hi_b = (lane >= ATT_HD).astype(BF16)
    lo = lax.broadcasted_iota(jnp.int32, (w, LANES), 1) < ATT_HD
    col0 = lax.broadcasted_iota(jnp.int32, (2 * w, LANES), 1) == 0
    upper = lax.broadcasted_iota(jnp.int32, (2 * w, LANES), 0) < w
    not_row0 = (lax.broadcasted_iota(jnp.int32, (BF16_ROWS, LANES), 0) > 0).astype(BF16)

    for j in range(ATT_HEADS // 2):
        kvh = j // (ATT_GROUP // 2)
        cs = slice(kvh * LANES, (kvh + 1) * LANES)
        if i == 0:
            kk = jnp.concatenate([kp_ref[:, cs], kc_ref[0:w, cs]], axis=0)
            vv = jnp.concatenate([vp_ref[0:BF16_ROWS, cs] * not_row0, vp_ref[BF16_ROWS:, cs], vc_ref[0:w, cs]], axis=0)
        else:
            r0 = (i - 1) * w
            kk = kc_ref[r0:r0 + 2 * w, cs]
            vv = jnp.concatenate([vc_ref[r0:r0 + BF16_ROWS, cs] * not_row0,
                                  vc_ref[r0 + BF16_ROWS:r0 + 2 * w, cs]], axis=0)
        qc = q_ref[i * w:(i + 1) * w, j * LANES:(j + 1) * LANES]
        qs = jnp.concatenate([qc * lo_b, qc * hi_b], axis=0)
        s = _dot_nt(qs, kk) + bias_ref[j]
        s_prev, s_cur = s[:, :w], s[:, w:]
        if i == 0:
            s_prev = s_prev + jnp.where(first_tile, NEG, 0.0)
        sink = jnp.where(upper, sink_ref[2 * j] * LOG2E, sink_ref[2 * j + 1] * LOG2E)
        s_prev = jnp.where(col0, sink, s_prev)
        m = jnp.max(jnp.maximum(s_prev, s_cur), axis=-1, keepdims=True)
        e_prev = jnp.exp2(s_prev - m)
        e_cur = jnp.exp2(s_cur - m)
        den = jnp.sum(e_prev + e_cur, axis=-1, keepdims=True)
        e = jnp.concatenate([e_prev.astype(BF16), e_cur.astype(BF16)], axis=1)
        o2 = _dot(e, vv) * (1.0 / den)
        o_ref[i * w:(i + 1) * w, j * LANES:(j + 1) * LANES] = jnp.where(lo, o2[:w], o2[w:]).astype(BF16)


def _seq_kernel(sink_ref, hq_ref, hk_ref, hlf_ref, hv_ref, hg_ref, tri_ref, lvl_ref,
                aq_ref, kc_ref, kp_ref, vc_ref, vp_ref, bias_ref, oa_ref, ob_ref, st_ref, *, nsub):
    first_tile = pl.program_id(1) == 0

    @pl.when(first_tile)
    def _():
        st_ref[...] = jnp.zeros_like(st_ref)

    level = lvl_ref[...]
    for i in range(nsub):
        _hgrn_tile(i, hq_ref, hk_ref, hlf_ref, hv_ref, hg_ref, tri_ref, level, oa_ref, st_ref)
        _swa_block(i, first_tile, sink_ref, aq_ref, kc_ref, kp_ref, vc_ref, vp_ref, bias_ref, ob_ref)


def _hgrn_consts():
    t = HG_TILE
    row = np.arange(t)[:, None]
    col = np.arange(t)[None, :]
    tri = (col <= row).astype(np.float32)
    ci, cj = row // HG_CHUNK, col // HG_CHUNK
    level = np.where((ci == cj) & (col <= row), 1, np.where((ci == cj + 1) & (cj % 2 == 0), 2, 0))
    return jnp.asarray(np.concatenate([tri, tri], axis=1), dtype=BF16), jnp.asarray(level, dtype=jnp.int32)


def _swa_bias():
    w = WINDOW
    i = np.arange(w)[:, None]
    j = np.arange(2 * w)[None, :]
    delta = (i + w - j).astype(np.float32)
    band = (delta >= 0) & (delta < w)
    assert not band[:, 0].any()
    slopes = 2.0 ** (-8.0 * np.arange(1, ATT_HEADS + 1, dtype=np.float32) / ATT_HEADS)
    per_head = np.where(band[None], -slopes[:, None, None] * delta[None] * np.float32(LOG2E), np.float32(NEG))
    return jnp.asarray(per_head.astype(np.float32).reshape(ATT_HEADS // 2, 2 * w, 2 * w))


def _seq(sinks, hq, hk, hlf, hv, hg, aq, ak, av, ts):
    b, s, _ = hq.shape
    tri, level = _hgrn_consts()
    bias = _swa_bias()
    nprev = ts // WINDOW
    wide = pl.BlockSpec((None, ts, HG_WIDTH), lambda bi, si, *_: (bi, si, 0))
    cur = pl.BlockSpec((None, ts, 2 * LANES), lambda bi, si, *_: (bi, si, 0))
    prev = pl.BlockSpec((None, WINDOW, 2 * LANES), lambda bi, si, *_: (bi, jnp.maximum(si * nprev - 1, 0), 0))
    full = lambda a: pl.BlockSpec(a.shape, lambda bi, si, *_: (0,) * a.ndim)
    return pl.pallas_call(
        functools.partial(_seq_kernel, nsub=ts // HG_TILE),
        grid_spec=pltpu.PrefetchScalarGridSpec(
            num_scalar_prefetch=1,
            grid=(b, s // ts),
            in_specs=[wide, wide, wide, wide, wide, full(tri), full(level),
                      wide, cur, prev, cur, prev, full(bias)],
            out_specs=[wide, wide],
            scratch_shapes=[pltpu.VMEM((HG_HEADS, HG_DV, HG_DK), F32)],
        ),
        out_shape=[jax.ShapeDtypeStruct((b, s, HG_WIDTH), BF16), jax.ShapeDtypeStruct((b, s, ATT_WIDTH), BF16)],
        compiler_params=pltpu.CompilerParams(dimension_semantics=("arbitrary", "arbitrary"),
                                             vmem_limit_bytes=VMEM_LIMIT),
        name="seq",
    )(sinks, hq, hk, hlf, hv, hg, tri, level, aq, ak, ak, av, av, bias)


def _post_kernel(x_ref, oa_ref, ob_ref, ga_ref, gb_ref, wa_ref, wb_ref, wo_ref, g2_ref,
                 wu_ref, cw_ref, cb_ref, wd_ref, o_ref, x1_ref, h2_ref, gs_ref, carry_ref, act_ref, *, tm):
    nchunk = D_FF // FF_CHUNK
    pad = 8

    @pl.when(pl.program_id(1) == 0)
    def _():
        carry_ref[...] = jnp.zeros_like(carry_ref)

    mixed = (ga_ref[...].astype(F32) * _dot(oa_ref[...], wa_ref[...])
             + gb_ref[...].astype(F32) * _dot(ob_ref[...], wb_ref[...]))
    x1_ref[...] = x_ref[...] + _dot(mixed.astype(BF16), wo_ref[...])
    h2_ref[...] = _rms(x1_ref[...], g2_ref[...]).astype(BF16)

    h2 = h2_ref[...]
    for c in range(nchunk):
        cs = slice(c * FF_CHUNK, (c + 1) * FF_CHUNK)
        gate = _dot(h2, wu_ref[:, cs])
        val = _dot(h2, wu_ref[:, D_FF + c * FF_CHUNK:D_FF + (c + 1) * FF_CHUNK])
        gs_ref[0:pad, :] = carry_ref[c]
        gs_ref[pad:pad + tm, :] = gate
        carry_ref[c] = gate[tm - pad:tm, :]
        conv = cb_ref[:, cs] + cw_ref[2:3, cs] * gate
        conv = conv + cw_ref[1:2, cs] * gs_ref[pad - 1:pad - 1 + tm, :]
        conv = conv + cw_ref[0:1, cs] * gs_ref[pad - 2:pad - 2 + tm, :]
        gelu = (0.5 * conv) * (1.0 + lax.erf(conv * (0.5 ** 0.5)))
        act_ref[:, cs] = (gelu * val).astype(BF16)
    o_ref[...] = x1_ref[...] + _dot(act_ref[...], wd_ref[...])


def _post(x, oa, ob, ga, gb, wa, wb, wo, g2, wu, cw, cb, wd, tm):
    b, s, _ = x.shape
    row = lambda c: pl.BlockSpec((None, tm, c), lambda bi, si: (bi, si, 0))
    full = lambda a: pl.BlockSpec(a.shape, lambda bi, si: (0,) * a.ndim, pipeline_mode=pl.Buffered(1))
    return pl.pallas_call(
        functools.partial(_post_kernel, tm=tm),
        grid=(b, s // tm),
        in_specs=[row(D_MODEL), row(HG_WIDTH), row(ATT_WIDTH), row(D_MODEL), row(D_MODEL),
                  full(wa), full(wb), full(wo), full(g2), full(wu), full(cw), full(cb), full(wd)],
        out_specs=row(D_MODEL),
        out_shape=jax.ShapeDtypeStruct((b, s, D_MODEL), F32),
        scratch_shapes=[pltpu.VMEM((tm, D_MODEL), F32),
                        pltpu.VMEM((tm, D_MODEL), BF16),
                        pltpu.VMEM((tm + 8, FF_CHUNK), F32),
                        pltpu.VMEM((D_FF // FF_CHUNK, 8, FF_CHUNK), F32),
                        pltpu.VMEM((tm, D_FF), BF16)],
        compiler_params=pltpu.CompilerParams(dimension_semantics=("arbitrary", "arbitrary"),
                                             vmem_limit_bytes=VMEM_LIMIT),
        name="post",
    )(x, oa, ob, ga, gb, wa, wb, wo, g2, wu, cw, cb, wd)


def _block_diag_ones(n, blk):
    idx = np.arange(n) // blk
    return jnp.asarray((idx[:, None] == idx[None, :]).astype(np.float32), dtype=BF16)


def kernel(x, norm1_g, w_in, hgrn_lb_logits, hgrn_out_g, q_norm_g, k_norm_g, attn_sinks,
           w_branch_a, w_branch_b, w_out, norm2_g, w_up, conv_w, conv_b, w_down):
    b, s, d = x.shape
    assert d == D_MODEL and w_in.shape == (1, D_MODEL, IN_COLS) and s % 512 == 0
    m = b * s
    tm = 512

    og = jnp.tile(hgrn_out_g[0], HG_HEADS).reshape(1, HG_WIDTH)
    qg = jnp.tile(q_norm_g[0], ATT_HEADS).reshape(1, ATT_WIDTH)
    kg = jnp.tile(k_norm_g[0], ATT_KV_HEADS).reshape(1, ATT_KV_WIDTH)
    hq, hk, hlf, hv, hg, aq, ak, av, ga, gb = _inproj(
        x.reshape(m, d), norm1_g[0].reshape(1, d), w_in[0], hgrn_lb_logits, og, qg, kg,
        _block_diag_ones(ATT_WIDTH, ATT_HD), _block_diag_ones(ATT_KV_WIDTH, ATT_HD), tm)

    r3 = lambda a: a.reshape(b, s, a.shape[-1])
    o_a, o_b = _seq(attn_sinks[0], r3(hq), r3(hk), r3(hlf), r3(hv), r3(hg), r3(aq), r3(ak), r3(av), 512)

    return _post(x, o_a, o_b, r3(ga), r3(gb),
                 w_branch_a[0].astype(BF16), w_branch_b[0].astype(BF16), w_out[0].astype(BF16),
                 norm2_g[0].reshape(1, d), w_up[0].astype(BF16),
                 conv_w[0], conv_b[0].reshape(1, D_FF), w_down[0].astype(BF16), tm)
```

```python
import functools

import numpy as np
import jax
import jax.numpy as jnp
from jax import lax
from jax.experimental import pallas as pl
from jax.experimental.pallas import tpu as pltpu

F32 = jnp.float32
BF16 = jnp.bfloat16

D_MODEL = 1024
HG_HEADS = 4
HG_DK = 128
HG_DV = 128
HG_WIDTH = HG_HEADS * HG_DK
HG_CHUNK = 32
ATT_HEADS = 8
ATT_KV_HEADS = 2
ATT_GROUP = ATT_HEADS // ATT_KV_HEADS
ATT_HD = 64
ATT_WIDTH = ATT_HEADS * ATT_HD
ATT_KV_WIDTH = ATT_KV_HEADS * ATT_HD
WINDOW = 128
D_FF = 2816
CONV_W = 3
EPS = 1e-6
NEG = -1e30
LOG2E = 1.4426950408889634

LANES = 128
BF16_ROWS = 16
HG_TILE = 128
FF_CHUNK = 256
VMEM_LIMIT = 56 * 1024 * 1024

C_HQ, C_HF, C_HI, C_HG = 0, HG_WIDTH, 2 * HG_WIDTH, 3 * HG_WIDTH
C_AQ = 4 * HG_WIDTH
C_AKV = C_AQ + ATT_WIDTH
C_GA = C_AKV + 2 * ATT_KV_WIDTH
C_GB = C_GA + D_MODEL
IN_COLS = C_GB + D_MODEL


def _dot(a, b):
    return jnp.dot(a, b, preferred_element_type=F32)


def _dot_nt(a, b):
    return lax.dot_general(a, b, (((1,), (1,)), ((), ())), preferred_element_type=F32)


def _dot_tn(a, b):
    return lax.dot_general(a, b, (((0,), (0,)), ((), ())), preferred_element_type=F32)


def _rms(x, g):
    ms = jnp.mean(x * x, axis=-1, keepdims=True)
    return (x * lax.rsqrt(ms + EPS)) * g


def _inproj_kernel(x_ref, g1_ref, w_ref, lbl_ref, og_ref, qg_ref, kg_ref, bdq_ref, bdk_ref,
                   hq_ref, hk_ref, hlf_ref, hv_ref, hg_ref, aq_ref, ak_ref, av_ref, ga_ref, gb_ref, wb_ref):
    @pl.when(pl.program_id(0) == 0)
    def _():
        wb_ref[...] = w_ref[...].astype(BF16)

    h = _rms(x_ref[...], g1_ref[...]).astype(BF16)

    def proj(c0, width):
        return _dot(h, wb_ref[:, c0:c0 + width])

    lbl = lbl_ref[...]
    e = jnp.exp(lbl - jnp.max(lbl, axis=0, keepdims=True))
    lb = e[0:1] / jnp.sum(e, axis=0, keepdims=True)

    q_pre = proj(C_HQ, HG_WIDTH)
    hq_ref[...] = (q_pre * jax.nn.sigmoid(q_pre)) * (HG_DK ** -0.5)
    f = lb + (1.0 - lb) * jax.nn.sigmoid(proj(C_HF, HG_WIDTH))
    hk_ref[...] = 1.0 - f
    hlf_ref[...] = jnp.log2(f)
    hv_ref[...] = proj(C_HI, HG_WIDTH).astype(BF16)
    g_pre = proj(C_HG, HG_WIDTH)
    hg_ref[...] = ((g_pre * jax.nn.sigmoid(g_pre)) * og_ref[...]).astype(BF16)

    a_q = proj(C_AQ, ATT_WIDTH)
    ms_q = _dot((a_q * a_q).astype(BF16), bdq_ref[...]) * (1.0 / ATT_HD)
    aq_ref[...] = (((a_q * lax.rsqrt(ms_q + EPS)) * qg_ref[...]) * (LOG2E * ATT_HD ** -0.5)).astype(BF16)

    kv = proj(C_AKV, 2 * ATT_KV_WIDTH)
    a_k = kv[:, :ATT_KV_WIDTH]
    ms_k = _dot((a_k * a_k).astype(BF16), bdk_ref[...]) * (1.0 / ATT_HD)
    lo = lax.broadcasted_iota(jnp.int32, (1, LANES), 1) < ATT_HD
    for src, dst in (((a_k * lax.rsqrt(ms_k + EPS)) * kg_ref[...], ak_ref), (kv[:, ATT_KV_WIDTH:], av_ref)):
        swapped = pltpu.roll(src, ATT_HD, axis=1)
        dst[:, :LANES] = jnp.where(lo, src, swapped).astype(BF16)
        dst[:, LANES:] = jnp.where(lo, swapped, src).astype(BF16)

    for part in range(2):
        w = D_MODEL // 2
        ga_ref[:, part * w:(part + 1) * w] = jax.nn.sigmoid(proj(C_GA + part * w, w)).astype(BF16)
        gb_ref[:, part * w:(part + 1) * w] = jax.nn.sigmoid(proj(C_GB + part * w, w)).astype(BF16)


def _inproj(x2, g1, w, lbl, og, qg, kg, bdq, bdk, tm):
    m = x2.shape[0]
    row = lambda c: pl.BlockSpec((tm, c), lambda i: (i, 0))
    full = lambda a: pl.BlockSpec(a.shape, lambda i: (0,) * a.ndim)
    outs = [
        (HG_WIDTH, F32), (HG_WIDTH, F32), (HG_WIDTH, F32), (HG_WIDTH, BF16), (HG_WIDTH, BF16),
        (ATT_WIDTH, BF16), (2 * LANES, BF16), (2 * LANES, BF16), (D_MODEL, BF16), (D_MODEL, BF16),
    ]
    return pl.pallas_call(
        _inproj_kernel,
        grid=(m // tm,),
        in_specs=[row(D_MODEL), full(g1),
                  pl.BlockSpec(w.shape, lambda i: (0, 0), pipeline_mode=pl.Buffered(1)),
                  full(lbl), full(og), full(qg), full(kg), full(bdq), full(bdk)],
        out_specs=[row(c) for c, _ in outs],
        out_shape=[jax.ShapeDtypeStruct((m, c), dt) for c, dt in outs],
        scratch_shapes=[pltpu.VMEM(w.shape, BF16)],
        compiler_params=pltpu.CompilerParams(dimension_semantics=("arbitrary",), vmem_limit_bytes=VMEM_LIMIT),
        name="inproj",
    )(x2, g1, w, lbl, og, qg, kg, bdq, bdk)


def _hgrn_head(q, k, v, g, sg, st, level):
    c = HG_CHUNK
    n = HG_TILE // c
    qs = [q[i * c:(i + 1) * c] for i in range(n)]
    ks = [k[i * c:(i + 1) * c] for i in range(n)]
    gs = [g[i * c:(i + 1) * c] for i in range(n)]
    mid = [gc[c // 2 - 1:c // 2] for gc in gs]
    end = [gc[c - 1:c] for gc in gs]
    p2 = jnp.exp2
    cat = lambda parts: jnp.concatenate([p.astype(BF16) for p in parts], axis=0)
    zero = jnp.zeros((c, HG_DK), BF16)

    s1 = _dot_nt(cat([qs[i] * p2(gs[i] - mid[i]) for i in range(n)]),
                 cat([ks[i] * p2(mid[i] - gs[i]) for i in range(n)]))
    k2_0 = ks[0] * p2(end[0] - gs[0])
    k2_2 = ks[2] * p2(end[2] - gs[2])
    q2_1 = qs[1] * p2(gs[1] - end[0])
    q2_3 = qs[3] * p2(gs[3] - end[2])
    s2 = _dot_nt(cat([zero, q2_1, zero, q2_3]), cat([k2_0, zero, k2_2, zero]))
    k3_0 = k2_0 * p2(end[1] - end[0])
    k3_1 = ks[1] * p2(end[1] - gs[1])
    q3_2 = qs[2] * p2(gs[2] - end[1])
    q3_3 = q2_3 * p2(end[2] - end[1])
    s3 = _dot_nt(cat([zero, zero, q3_2, q3_3]), cat([k3_0, k3_1, zero, zero]))
    scores = jnp.where(level == 1, s1, jnp.where(level == 2, s2, s3))
    o = _dot(scores.astype(BF16), v)

    d_half = p2(end[3] - end[1])
    k_in = cat([k3_0 * d_half, k3_1 * d_half, k2_2 * p2(end[3] - end[2]), ks[3] * p2(end[3] - gs[3])])
    q_in = cat([qs[0] * p2(gs[0]), q2_1 * p2(end[0]), q3_2 * p2(end[1]), q3_3 * p2(end[1])])
    o = o + _dot_nt(q_in, st.astype(BF16))
    st_new = st * p2(end[3]) + _dot_tn(v, k_in)

    ms = jnp.mean(o * o, axis=-1, keepdims=True)
    return (o * lax.rsqrt(ms + EPS)) * sg, st_new


def _hgrn_prefix(i, lf_ref, tri_ref):
    lf = lf_ref[i * HG_TILE:(i + 1) * HG_TILE, :]
    lf_hi = lf.astype(BF16)
    lf_lo = (lf - lf_hi.astype(F32)).astype(BF16)
    return _dot(tri_ref[...], jnp.concatenate([lf_hi, lf_lo], axis=0))


def _hgrn_step(i, h, g_all, q_ref, k_ref, v_ref, sg_ref, level, o_ref, st_ref):
    rs = slice(i * HG_TILE, (i + 1) * HG_TILE)
    cs = slice(h * HG_DK, (h + 1) * HG_DK)
    o, st_new = _hgrn_head(q_ref[rs, cs], k_ref[rs, cs], v_ref[rs, cs], g_all[:, cs],
                           sg_ref[rs, cs].astype(F32), st_ref[h], level)
    st_ref[h] = st_new
    o_ref[rs, cs] = o.astype(BF16)


def _swa_step(i, j, first_tile, sink_ref, q_ref, kc_ref, kp_ref, vc_ref, vp_ref, bias_ref, o_ref):
    w = WINDOW
    lane = lax.broadcasted_iota(jnp.int32, (1, LANES), 1)
    lo_b = (lane < ATT_HD).astype(BF16)
    hi_b = (lane >= ATT_HD).astype(BF16)
    lo = lax.broadcasted_iota(jnp.int32, (w, LANES), 1) < ATT_HD
    col0 = lax.broadcasted_iota(jnp.int32, (2 * w, LANES), 1) == 0
    upper = lax.broadcasted_iota(jnp.int32, (2 * w, LANES), 0) < w
    not_row0 = (lax.broadcasted_iota(jnp.int32, (BF16_ROWS, LANES), 0) > 0).astype(BF16)

    kvh = j // (ATT_GROUP // 2)
    cs = slice(kvh * LANES, (kvh + 1) * LANES)
    if i == 0:
        kk = jnp.concatenate([kp_ref[:, cs], kc_ref[0:w, cs]], axis=0)
        vv = jnp.concatenate([vp_ref[0:BF16_ROWS, cs] * not_row0, vp_ref[BF16_ROWS:, cs], vc_ref[0:w, cs]], axis=0)
    else:
        r0 = (i - 1) * w
        kk = kc_ref[r0:r0 + 2 * w, cs]
        vv = jnp.concatenate([vc_ref[r0:r0 + BF16_ROWS, cs] * not_row0,
                              vc_ref[r0 + BF16_ROWS:r0 + 2 * w, cs]], axis=0)
    qc = q_ref[i * w:(i + 1) * w, j * LANES:(j + 1) * LANES]
    qs = jnp.concatenate([qc * lo_b, qc * hi_b], axis=0)
    s = _dot_nt(qs, kk) + bias_ref[j]
    s_prev, s_cur = s[:, :w], s[:, w:]
    if i == 0:
        s_prev = s_prev + jnp.where(first_tile, NEG, 0.0)
    sink = jnp.where(upper, sink_ref[2 * j] * LOG2E, sink_ref[2 * j + 1] * LOG2E)
    s_prev = jnp.where(col0, sink, s_prev)
    m = jnp.max(jnp.maximum(s_prev, s_cur), axis=-1, keepdims=True)
    e_prev = jnp.exp2(s_prev - m)
    e_cur = jnp.exp2(s_cur - m)
    den = jnp.sum(e_prev + e_cur, axis=-1, keepdims=True)
    e = jnp.concatenate([e_prev.astype(BF16), e_cur.astype(BF16)], axis=1)
    o2 = _dot(e, vv) * (1.0 / den)
    o_ref[i * w:(i + 1) * w, j * LANES:(j + 1) * LANES] = jnp.where(lo, o2[:w], o2[w:]).astype(BF16)


def _seq_kernel(sink_ref, hq_ref, hk_ref, hlf_ref, hv_ref, hg_ref, tri_ref, lvl_ref,
                aq_ref, kc_ref, kp_ref, vc_ref, vp_ref, bias_ref, oa_ref, ob_ref, st_ref, *, nsub):
    first_tile = pl.program_id(1) == 0

    @pl.when(first_tile)
    def _():
        st_ref[...] = jnp.zeros_like(st_ref)

    level = lvl_ref[...]
    assert HG_HEADS == ATT_HEADS // 2 and HG_TILE == WINDOW
    for i in range(nsub):
        g_all = _hgrn_prefix(i, hlf_ref, tri_ref)
        for h in range(HG_HEADS):
            _hgrn_step(i, h, g_all, hq_ref, hk_ref, hv_ref, hg_ref, level, oa_ref, st_ref)
            _swa_step(i, h, first_tile, sink_ref, aq_ref, kc_ref, kp_ref, vc_ref, vp_ref, bias_ref, ob_ref)


def _hgrn_consts():
    t = HG_TILE
    row = np.arange(t)[:, None]
    col = np.arange(t)[None, :]
    tri = (col <= row).astype(np.float32)
    ci, cj = row // HG_CHUNK, col // HG_CHUNK
    level = np.where((ci == cj) & (col <= row), 1, np.where((ci == cj + 1) & (cj % 2 == 0), 2, 0))
    return jnp.asarray(np.concatenate([tri, tri], axis=1), dtype=BF16), jnp.asarray(level, dtype=jnp.int32)


def _swa_bias():
    w = WINDOW
    i = np.arange(w)[:, None]
    j = np.arange(2 * w)[None, :]
    delta = (i + w - j).astype(np.float32)
    band = (delta >= 0) & (delta < w)
    assert not band[:, 0].any()
    slopes = 2.0 ** (-8.0 * np.arange(1, ATT_HEADS + 1, dtype=np.float32) / ATT_HEADS)
    per_head = np.where(band[None], -slopes[:, None, None] * delta[None] * np.float32(LOG2E), np.float32(NEG))
    return jnp.asarray(per_head.astype(np.float32).reshape(ATT_HEADS // 2, 2 * w, 2 * w))


def _seq(sinks, hq, hk, hlf, hv, hg, aq, ak, av, ts):
    b, s, _ = hq.shape
    tri, level = _hgrn_consts()
    bias = _swa_bias()
    nprev = ts // WINDOW
    wide = pl.BlockSpec((None, ts, HG_WIDTH), lambda bi, si, *_: (bi, si, 0))
    cur = pl.BlockSpec((None, ts, 2 * LANES), lambda bi, si, *_: (bi, si, 0))
    prev = pl.BlockSpec((None, WINDOW, 2 * LANES), lambda bi, si, *_: (bi, jnp.maximum(si * nprev - 1, 0), 0))
    full = lambda a: pl.BlockSpec(a.shape, lambda bi, si, *_: (0,) * a.ndim)
    return pl.pallas_call(
        functools.partial(_seq_kernel, nsub=ts // HG_TILE),
        grid_spec=pltpu.PrefetchScalarGridSpec(
            num_scalar_prefetch=1,
            grid=(b, s // ts),
            in_specs=[wide, wide, wide, wide, wide, full(tri), full(level),
                      wide, cur, prev, cur, prev, full(bias)],
            out_specs=[wide, wide],
            scratch_shapes=[pltpu.VMEM((HG_HEADS, HG_DV, HG_DK), F32)],
        ),
        out_shape=[jax.ShapeDtypeStruct((b, s, HG_WIDTH), BF16), jax.ShapeDtypeStruct((b, s, ATT_WIDTH), BF16)],
        compiler_params=pltpu.CompilerParams(dimension_semantics=("arbitrary", "arbitrary"),
                                             vmem_limit_bytes=VMEM_LIMIT),
        name="seq",
    )(sinks, hq, hk, hlf, hv, hg, tri, level, aq, ak, ak, av, av, bias)


def _post_kernel(x_ref, oa_ref, ob_ref, ga_ref, gb_ref, wa_ref, wb_ref, wo_ref, g2_ref,
                 wu_ref, cw_ref, cb_ref, wd_ref, o_ref, x1_ref, h2_ref, gs_ref, carry_ref, act_ref, *, tm):
    pad = 8

    @pl.when(pl.program_id(1) == 0)
    def _():
        carry_ref[...] = jnp.zeros_like(carry_ref)

    mixed = (ga_ref[...].astype(F32) * _dot(oa_ref[...], wa_ref[...])
             + gb_ref[...].astype(F32) * _dot(ob_ref[...], wb_ref[...]))
    x1_ref[...] = x_ref[...] + _dot(mixed.astype(BF16), wo_ref[...])
    h2_ref[...] = _rms(x1_ref[...], g2_ref[...]).astype(BF16)

    h2 = h2_ref[...]
    for c0 in range(0, D_FF, FF_CHUNK):
        cw = min(FF_CHUNK, D_FF - c0)
        cs = slice(c0, c0 + cw)
        gate = _dot(h2, wu_ref[:, cs])
        val = _dot(h2, wu_ref[:, D_FF + c0:D_FF + c0 + cw])
        gs_ref[0:pad, :cw] = carry_ref[:, cs]
        gs_ref[pad:pad + tm, :cw] = gate
        carry_ref[:, cs] = gate[tm - pad:tm, :]
        conv = cb_ref[:, cs] + cw_ref[2:3, cs] * gate
        conv = conv + cw_ref[1:2, cs] * gs_ref[pad - 1:pad - 1 + tm, :cw]
        conv = conv + cw_ref[0:1, cs] * gs_ref[pad - 2:pad - 2 + tm, :cw]
        gelu = (0.5 * conv) * (1.0 + lax.erf(conv * (0.5 ** 0.5)))
        act_ref[:, cs] = (gelu * val).astype(BF16)
    o_ref[...] = x1_ref[...] + _dot(act_ref[...], wd_ref[...])


def _post(x, oa, ob, ga, gb, wa, wb, wo, g2, wu, cw, cb, wd, tm):
    b, s, _ = x.shape
    row = lambda c: pl.BlockSpec((None, tm, c), lambda bi, si: (bi, si, 0))
    full = lambda a: pl.BlockSpec(a.shape, lambda bi, si: (0,) * a.ndim, pipeline_mode=pl.Buffered(1))
    return pl.pallas_call(
        functools.partial(_post_kernel, tm=tm),
        grid=(b, s // tm),
        in_specs=[row(D_MODEL), row(HG_WIDTH), row(ATT_WIDTH), row(D_MODEL), row(D_MODEL),
                  full(wa), full(wb), full(wo), full(g2), full(wu), full(cw), full(cb), full(wd)],
        out_specs=row(D_MODEL),
        out_shape=jax.ShapeDtypeStruct((b, s, D_MODEL), F32),
        scratch_shapes=[pltpu.VMEM((tm, D_MODEL), F32),
                        pltpu.VMEM((tm, D_MODEL), BF16),
                        pltpu.VMEM((tm + 8, FF_CHUNK), F32),
                        pltpu.VMEM((8, D_FF), F32),
                        pltpu.VMEM((tm, D_FF), BF16)],
        compiler_params=pltpu.CompilerParams(dimension_semantics=("arbitrary", "arbitrary"),
                                             vmem_limit_bytes=VMEM_LIMIT),
        name="post",
    )(x, oa, ob, ga, gb, wa, wb, wo, g2, wu, cw, cb, wd)


def _block_diag_ones(n, blk):
    idx = np.arange(n) // blk
    return jnp.asarray((idx[:, None] == idx[None, :]).astype(np.float32), dtype=BF16)


def kernel(x, norm1_g, w_in, hgrn_lb_logits, hgrn_out_g, q_norm_g, k_norm_g, attn_sinks,
           w_branch_a, w_branch_b, w_out, norm2_g, w_up, conv_w, conv_b, w_down):
    b, s, d = x.shape
    assert d == D_MODEL and w_in.shape == (1, D_MODEL, IN_COLS) and s % 512 == 0
    m = b * s
    tm = 512

    og = jnp.tile(hgrn_out_g[0], HG_HEADS).reshape(1, HG_WIDTH)
    qg = jnp.tile(q_norm_g[0], ATT_HEADS).reshape(1, ATT_WIDTH)
    kg = jnp.tile(k_norm_g[0], ATT_KV_HEADS).reshape(1, ATT_KV_WIDTH)
    hq, hk, hlf, hv, hg, aq, ak, av, ga, gb = _inproj(
        x.reshape(m, d), norm1_g[0].reshape(1, d), w_in[0], hgrn_lb_logits, og, qg, kg,
        _block_diag_ones(ATT_WIDTH, ATT_HD), _block_diag_ones(ATT_KV_WIDTH, ATT_HD), tm)

    r3 = lambda a: a.reshape(b, s, a.shape[-1])
    o_a, o_b = _seq(attn_sinks[0], r3(hq), r3(hk), r3(hlf), r3(hv), r3(hg), r3(aq), r3(ak), r3(av), 1024)

    return _post(x, o_a, o_b, r3(ga), r3(gb),
                 w_branch_a[0].astype(BF16), w_branch_b[0].astype(BF16), w_out[0].astype(BF16),
                 norm2_g[0].reshape(1, d), w_up[0].astype(BF16),
                 conv_w[0], conv_b[0].reshape(1, D_FF), w_down[0].astype(BF16), tm)
```

```python
import functools

import numpy as np
import jax
import jax.numpy as jnp
from jax import lax
from jax.experimental import pallas as pl
from jax.experimental.pallas import tpu as pltpu

F32 = jnp.float32
BF16 = jnp.bfloat16

D_MODEL = 1024
HG_HEADS = 4
HG_DK = 128
HG_DV = 128
HG_WIDTH = HG_HEADS * HG_DK
HG_CHUNK = 32
ATT_HEADS = 8
ATT_KV_HEADS = 2
ATT_GROUP = ATT_HEADS // ATT_KV_HEADS
ATT_HD = 64
ATT_WIDTH = ATT_HEADS * ATT_HD
ATT_KV_WIDTH = ATT_KV_HEADS * ATT_HD
WINDOW = 128
D_FF = 2816
CONV_W = 3
EPS = 1e-6
NEG = -1e30
LOG2E = 1.4426950408889634

LANES = 128
BF16_ROWS = 16
HG_TILE = 128
FF_CHUNK = 256
VMEM_LIMIT = 56 * 1024 * 1024

C_HQ, C_HF, C_HI, C_HG = 0, HG_WIDTH, 2 * HG_WIDTH, 3 * HG_WIDTH
C_AQ = 4 * HG_WIDTH
C_AKV = C_AQ + ATT_WIDTH
C_GA = C_AKV + 2 * ATT_KV_WIDTH
C_GB = C_GA + D_MODEL
IN_COLS = C_GB + D_MODEL


def _dot(a, b):
    return jnp.dot(a, b, preferred_element_type=F32)


def _dot_nt(a, b):
    return lax.dot_general(a, b, (((1,), (1,)), ((), ())), preferred_element_type=F32)


def _dot_tn(a, b):
    return lax.dot_general(a, b, (((0,), (0,)), ((), ())), preferred_element_type=F32)


def _rms(x, g):
    ms = jnp.mean(x * x, axis=-1, keepdims=True)
    return (x * lax.rsqrt(ms + EPS)) * g


def _inproj_kernel(x_ref, g1_ref, w_ref, lbl_ref, og_ref, qg_ref, kg_ref, bdq_ref, bdk_ref,
                   hq_ref, hk_ref, hlf_ref, hv_ref, hg_ref, aq_ref, ak_ref, av_ref, ga_ref, gb_ref, wb_ref):
    @pl.when(pl.program_id(0) == 0)
    def _():
        wb_ref[...] = w_ref[...].astype(BF16)

    h = _rms(x_ref[...], g1_ref[...]).astype(BF16)

    def proj(c0, width):
        return _dot(h, wb_ref[:, c0:c0 + width])

    lbl = lbl_ref[...]
    e = jnp.exp(lbl - jnp.max(lbl, axis=0, keepdims=True))
    lb = e[0:1] / jnp.sum(e, axis=0, keepdims=True)

    q_pre = proj(C_HQ, HG_WIDTH)
    hq_ref[...] = (q_pre * jax.nn.sigmoid(q_pre)) * (HG_DK ** -0.5)
    f = lb + (1.0 - lb) * jax.nn.sigmoid(proj(C_HF, HG_WIDTH))
    hk_ref[...] = 1.0 - f
    hlf_ref[...] = jnp.log2(f)
    g_pre = proj(C_HG, HG_WIDTH)
    hg_ref[...] = ((g_pre * jax.nn.sigmoid(g_pre)) * og_ref[...]).astype(BF16)

    def gate(dst, c_w, part):
        w = D_MODEL // 2
        dst[:, part * w:(part + 1) * w] = jax.nn.sigmoid(proj(c_w + part * w, w)).astype(BF16)

    assert C_AKV == C_AQ + ATT_WIDTH
    aqkv = proj(C_AQ, ATT_WIDTH + 2 * ATT_KV_WIDTH)
    a_q, kv = aqkv[:, :ATT_WIDTH], aqkv[:, ATT_WIDTH:]
    gate(ga_ref, C_GA, 0)
    ms_q = _dot((a_q * a_q).astype(BF16), bdq_ref[...]) * (1.0 / ATT_HD)
    aq_ref[...] = (((a_q * lax.rsqrt(ms_q + EPS)) * qg_ref[...]) * (LOG2E * ATT_HD ** -0.5)).astype(BF16)
    gate(ga_ref, C_GA, 1)
    a_k = kv[:, :ATT_KV_WIDTH]
    ms_k = _dot((a_k * a_k).astype(BF16), bdk_ref[...]) * (1.0 / ATT_HD)
    lo = lax.broadcasted_iota(jnp.int32, (1, LANES), 1) < ATT_HD
    for src, dst in (((a_k * lax.rsqrt(ms_k + EPS)) * kg_ref[...], ak_ref), (kv[:, ATT_KV_WIDTH:], av_ref)):
        swapped = pltpu.roll(src, ATT_HD, axis=1)
        dst[:, :LANES] = jnp.where(lo, src, swapped).astype(BF16)
        dst[:, LANES:] = jnp.where(lo, swapped, src).astype(BF16)
    gate(gb_ref, C_GB, 0)
    gate(gb_ref, C_GB, 1)
    hv_ref[...] = proj(C_HI, HG_WIDTH).astype(BF16)


def _inproj(x2, g1, w, lbl, og, qg, kg, bdq, bdk, tm):
    m = x2.shape[0]
    row = lambda c: pl.BlockSpec((tm, c), lambda i: (i, 0))
    full = lambda a: pl.BlockSpec(a.shape, lambda i: (0,) * a.ndim)
    outs = [
        (HG_WIDTH, F32), (HG_WIDTH, F32), (HG_WIDTH, F32), (HG_WIDTH, BF16), (HG_WIDTH, BF16),
        (ATT_WIDTH, BF16), (2 * LANES, BF16), (2 * LANES, BF16), (D_MODEL, BF16), (D_MODEL, BF16),
    ]
    return pl.pallas_call(
        _inproj_kernel,
        grid=(m // tm,),
        in_specs=[row(D_MODEL), full(g1),
                  pl.BlockSpec(w.shape, lambda i: (0, 0), pipeline_mode=pl.Buffered(1)),
                  full(lbl), full(og), full(qg), full(kg), full(bdq), full(bdk)],
        out_specs=[row(c) for c, _ in outs],
        out_shape=[jax.ShapeDtypeStruct((m, c), dt) for c, dt in outs],
        scratch_shapes=[pltpu.VMEM(w.shape, BF16)],
        compiler_params=pltpu.CompilerParams(dimension_semantics=("arbitrary",), vmem_limit_bytes=VMEM_LIMIT),
        name="inproj",
    )(x2, g1, w, lbl, og, qg, kg, bdq, bdk)


def _hgrn_head(q, k, v, g, sg, st, level):
    c = HG_CHUNK
    n = HG_TILE // c
    qs = [q[i * c:(i + 1) * c] for i in range(n)]
    ks = [k[i * c:(i + 1) * c] for i in range(n)]
    gs = [g[i * c:(i + 1) * c] for i in range(n)]
    mid = [gc[c // 2 - 1:c // 2] for gc in gs]
    end = [gc[c - 1:c] for gc in gs]
    p2 = jnp.exp2
    cat = lambda parts: jnp.concatenate([p.astype(BF16) for p in parts], axis=0)
    zero = jnp.zeros((c, HG_DK), BF16)

    s1 = _dot_nt(cat([qs[i] * p2(gs[i] - mid[i]) for i in range(n)]),
                 cat([ks[i] * p2(mid[i] - gs[i]) for i in range(n)]))
    k2_0 = ks[0] * p2(end[0] - gs[0])
    k2_2 = ks[2] * p2(end[2] - gs[2])
    q2_1 = qs[1] * p2(gs[1] - end[0])
    q2_3 = qs[3] * p2(gs[3] - end[2])
    s2 = _dot_nt(cat([zero, q2_1, zero, q2_3]), cat([k2_0, zero, k2_2, zero]))
    k3_0 = k2_0 * p2(end[1] - end[0])
    k3_1 = ks[1] * p2(end[1] - gs[1])
    q3_2 = qs[2] * p2(gs[2] - end[1])
    q3_3 = q2_3 * p2(end[2] - end[1])
    s3 = _dot_nt(cat([zero, zero, q3_2, q3_3]), cat([k3_0, k3_1, zero, zero]))
    scores = jnp.where(level == 1, s1, jnp.where(level == 2, s2, s3))
    o = _dot(scores.astype(BF16), v)

    d_half = p2(end[3] - end[1])
    k_in = cat([k3_0 * d_half, k3_1 * d_half, k2_2 * p2(end[3] - end[2]), ks[3] * p2(end[3] - gs[3])])
    q_in = cat([qs[0] * p2(gs[0]), q2_1 * p2(end[0]), q3_2 * p2(end[1]), q3_3 * p2(end[1])])
    o = o + _dot_nt(q_in, st.astype(BF16))
    st_new = st * p2(end[3]) + _dot_tn(v, k_in)

    ms = jnp.mean(o * o, axis=-1, keepdims=True)
    return (o * lax.rsqrt(ms + EPS)) * sg, st_new


def _hgrn_prefix(i, lf_ref, tri_ref):
    lf = lf_ref[i * HG_TILE:(i + 1) * HG_TILE, :]
    lf_hi = lf.astype(BF16)
    lf_lo = (lf - lf_hi.astype(F32)).astype(BF16)
    return _dot(tri_ref[...], jnp.concatenate([lf_hi, lf_lo], axis=0))


def _hgrn_step(i, h, g_all, q_ref, k_ref, v_ref, sg_ref, level, o_ref, st_ref):
    rs = slice(i * HG_TILE, (i + 1) * HG_TILE)
    cs = slice(h * HG_DK, (h + 1) * HG_DK)
    o, st_new = _hgrn_head(q_ref[rs, cs], k_ref[rs, cs], v_ref[rs, cs], g_all[:, cs],
                           sg_ref[rs, cs].astype(F32), st_ref[h], level)
    st_ref[h] = st_new
    o_ref[rs, cs] = o.astype(BF16)


def _swa_step(i, j, first_tile, sink_ref, q_ref, kc_ref, kp_ref, vc_ref, vp_ref, bias_ref, o_ref):
    w = WINDOW
    lane = lax.broadcasted_iota(jnp.int32, (1, LANES), 1)
    lo_b = (lane < ATT_HD).astype(BF16)
    hi_b = (lane >= ATT_HD).astype(BF16)
    lo = lax.broadcasted_iota(jnp.int32, (w, LANES), 1) < ATT_HD
    col0 = lax.broadcasted_iota(jnp.int32, (2 * w, LANES), 1) == 0
    upper = lax.broadcasted_iota(jnp.int32, (2 * w, LANES), 0) < w
    not_row0 = (lax.broadcasted_iota(jnp.int32, (BF16_ROWS, LANES), 0) > 0).astype(BF16)

    kvh = j // (ATT_GROUP // 2)
    cs = slice(kvh * LANES, (kvh + 1) * LANES)
    if i == 0:
        kk = jnp.concatenate([kp_ref[:, cs], kc_ref[0:w, cs]], axis=0)
        vv = jnp.concatenate([vp_ref[0:BF16_ROWS, cs] * not_row0, vp_ref[BF16_ROWS:, cs], vc_ref[0:w, cs]], axis=0)
    else:
        r0 = (i - 1) * w
        kk = kc_ref[r0:r0 + 2 * w, cs]
        vv = jnp.concatenate([vc_ref[r0:r0 + BF16_ROWS, cs] * not_row0,
                              vc_ref[r0 + BF16_ROWS:r0 + 2 * w, cs]], axis=0)
    qc = q_ref[i * w:(i + 1) * w, j * LANES:(j + 1) * LANES]
    qs = jnp.concatenate([qc * lo_b, qc * hi_b], axis=0)
    s = _dot_nt(qs, kk) + bias_ref[j]
    s_prev, s_cur = s[:, :w], s[:, w:]
    if i == 0:
        s_prev = s_prev + jnp.where(first_tile, NEG, 0.0)
    sink = jnp.where(upper, sink_ref[2 * j] * LOG2E, sink_ref[2 * j + 1] * LOG2E)
    s_prev = jnp.where(col0, sink, s_prev)
    m = jnp.max(jnp.maximum(s_prev, s_cur), axis=-1, keepdims=True)
    e_prev = jnp.exp2(s_prev - m)
    e_cur = jnp.exp2(s_cur - m)
    den = jnp.sum(e_prev + e_cur, axis=-1, keepdims=True)
    e = jnp.concatenate([e_prev.astype(BF16), e_cur.astype(BF16)], axis=1)
    o2 = _dot(e, vv) * (1.0 / den)
    o_ref[i * w:(i + 1) * w, j * LANES:(j + 1) * LANES] = jnp.where(lo, o2[:w], o2[w:]).astype(BF16)


def _seq_kernel(sink_ref, hq_ref, hk_ref, hlf_ref, hv_ref, hg_ref, tri_ref, lvl_ref,
                aq_ref, kc_ref, kp_ref, vc_ref, vp_ref, bias_ref, oa_ref, ob_ref, st_ref, *, nsub):
    first_tile = pl.program_id(1) == 0

    @pl.when(first_tile)
    def _():
        st_ref[...] = jnp.zeros_like(st_ref)

    level = lvl_ref[...]
    assert HG_HEADS == ATT_HEADS // 2 and HG_TILE == WINDOW
    for i in range(nsub):
        g_all = _hgrn_prefix(i, hlf_ref, tri_ref)
        for h in range(HG_HEADS):
            _hgrn_step(i, h, g_all, hq_ref, hk_ref, hv_ref, hg_ref, level, oa_ref, st_ref)
            _swa_step(i, h, first_tile, sink_ref, aq_ref, kc_ref, kp_ref, vc_ref, vp_ref, bias_ref, ob_ref)


def _hgrn_consts():
    t = HG_TILE
    row = np.arange(t)[:, None]
    col = np.arange(t)[None, :]
    tri = (col <= row).astype(np.float32)
    ci, cj = row // HG_CHUNK, col // HG_CHUNK
    level = np.where((ci == cj) & (col <= row), 1, np.where((ci == cj + 1) & (cj % 2 == 0), 2, 0))
    return jnp.asarray(np.concatenate([tri, tri], axis=1), dtype=BF16), jnp.asarray(level, dtype=jnp.int32)


def _swa_bias():
    w = WINDOW
    i = np.arange(w)[:, None]
    j = np.arange(2 * w)[None, :]
    delta = (i + w - j).astype(np.float32)
    band = (delta >= 0) & (delta < w)
    assert not band[:, 0].any()
    slopes = 2.0 ** (-8.0 * np.arange(1, ATT_HEADS + 1, dtype=np.float32) / ATT_HEADS)
    per_head = np.where(band[None], -slopes[:, None, None] * delta[None] * np.float32(LOG2E), np.float32(NEG))
    return jnp.asarray(per_head.astype(np.float32).reshape(ATT_HEADS // 2, 2 * w, 2 * w))


def _seq(sinks, hq, hk, hlf, hv, hg, aq, ak, av, ts):
    b, s, _ = hq.shape
    tri, level = _hgrn_consts()
    bias = _swa_bias()
    nprev = ts // WINDOW
    wide = pl.BlockSpec((None, ts, HG_WIDTH), lambda bi, si, *_: (bi, si, 0))
    cur = pl.BlockSpec((None, ts, 2 * LANES), lambda bi, si, *_: (bi, si, 0))
    prev = pl.BlockSpec((None, WINDOW, 2 * LANES), lambda bi, si, *_: (bi, jnp.maximum(si * nprev - 1, 0), 0))
    full = lambda a: pl.BlockSpec(a.shape, lambda bi, si, *_: (0,) * a.ndim)
    return pl.pallas_call(
        functools.partial(_seq_kernel, nsub=ts // HG_TILE),
        grid_spec=pltpu.PrefetchScalarGridSpec(
            num_scalar_prefetch=1,
            grid=(b, s // ts),
            in_specs=[wide, wide, wide, wide, wide, full(tri), full(level),
                      wide, cur, prev, cur, prev, full(bias)],
            out_specs=[wide, wide],
            scratch_shapes=[pltpu.VMEM((HG_HEADS, HG_DV, HG_DK), F32)],
        ),
        out_shape=[jax.ShapeDtypeStruct((b, s, HG_WIDTH), BF16), jax.ShapeDtypeStruct((b, s, ATT_WIDTH), BF16)],
        compiler_params=pltpu.CompilerParams(dimension_semantics=("arbitrary", "arbitrary"),
                                             vmem_limit_bytes=VMEM_LIMIT),
        name="seq",
    )(sinks, hq, hk, hlf, hv, hg, tri, level, aq, ak, ak, av, av, bias)


def _post_kernel(x_ref, oa_ref, ob_ref, ga_ref, gb_ref, wa_ref, wb_ref, wo_ref, g2_ref,
                 wu_ref, cw_ref, cb_ref, wd_ref, o_ref, x1_ref, h2_ref, gs_ref, carry_ref, act_ref, *, tm):
    pad = 8

    @pl.when(pl.program_id(1) == 0)
    def _():
        carry_ref[...] = jnp.zeros_like(carry_ref)

    mixed = (ga_ref[...].astype(F32) * _dot(oa_ref[...], wa_ref[...])
             + gb_ref[...].astype(F32) * _dot(ob_ref[...], wb_ref[...]))
    x1_ref[...] = x_ref[...] + _dot(mixed.astype(BF16), wo_ref[...])
    h2_ref[...] = _rms(x1_ref[...], g2_ref[...]).astype(BF16)

    h2 = h2_ref[...]
    for c0 in range(0, D_FF, FF_CHUNK):
        cw = min(FF_CHUNK, D_FF - c0)
        cs = slice(c0, c0 + cw)
        gate = _dot(h2, wu_ref[:, cs])
        val = _dot(h2, wu_ref[:, D_FF + c0:D_FF + c0 + cw])
        gs_ref[0:pad, :cw] = carry_ref[:, cs]
        gs_ref[pad:pad + tm, :cw] = gate
        carry_ref[:, cs] = gate[tm - pad:tm, :]
        conv = cb_ref[:, cs] + cw_ref[2:3, cs] * gate
        conv = conv + cw_ref[1:2, cs] * gs_ref[pad - 1:pad - 1 + tm, :cw]
        conv = conv + cw_ref[0:1, cs] * gs_ref[pad - 2:pad - 2 + tm, :cw]
        gelu = (0.5 * conv) * (1.0 + lax.erf(conv * (0.5 ** 0.5)))
        act_ref[:, cs] = (gelu * val).astype(BF16)
    o_ref[...] = x1_ref[...] + _dot(act_ref[...], wd_ref[...])


def _post(x, oa, ob, ga, gb, wa, wb, wo, g2, wu, cw, cb, wd, tm):
    b, s, _ = x.shape
    row = lambda c: pl.BlockSpec((None, tm, c), lambda bi, si: (bi, si, 0))
    full = lambda a: pl.BlockSpec(a.shape, lambda bi, si: (0,) * a.ndim, pipeline_mode=pl.Buffered(1))
    return pl.pallas_call(
        functools.partial(_post_kernel, tm=tm),
        grid=(b, s // tm),
        in_specs=[row(D_MODEL), row(HG_WIDTH), row(ATT_WIDTH), row(D_MODEL), row(D_MODEL),
                  full(wa), full(wb), full(wo), full(g2), full(wu), full(cw), full(cb), full(wd)],
        out_specs=row(D_MODEL),
        out_shape=jax.ShapeDtypeStruct((b, s, D_MODEL), F32),
        scratch_shapes=[pltpu.VMEM((tm, D_MODEL), F32),
                        pltpu.VMEM((tm, D_MODEL), BF16),
                        pltpu.VMEM((tm + 8, FF_CHUNK), F32),
                        pltpu.VMEM((8, D_FF), F32),
                        pltpu.VMEM((tm, D_FF), BF16)],
        compiler_params=pltpu.CompilerParams(dimension_semantics=("arbitrary", "arbitrary"),
                                             vmem_limit_bytes=VMEM_LIMIT),
        name="post",
    )(x, oa, ob, ga, gb, wa, wb, wo, g2, wu, cw, cb, wd)


def _block_diag_ones(n, blk):
    idx = np.arange(n) // blk
    return jnp.asarray((idx[:, None] == idx[None, :]).astype(np.float32), dtype=BF16)


def kernel(x, norm1_g, w_in, hgrn_lb_logits, hgrn_out_g, q_norm_g, k_norm_g, attn_sinks,
           w_branch_a, w_branch_b, w_out, norm2_g, w_up, conv_w, conv_b, w_down):
    b, s, d = x.shape
    assert d == D_MODEL and w_in.shape == (1, D_MODEL, IN_COLS) and s % 512 == 0
    m = b * s
    tm = 512

    og = jnp.tile(hgrn_out_g[0], HG_HEADS).reshape(1, HG_WIDTH)
    qg = jnp.tile(q_norm_g[0], ATT_HEADS).reshape(1, ATT_WIDTH)
    kg = jnp.tile(k_norm_g[0], ATT_KV_HEADS).reshape(1, ATT_KV_WIDTH)
    hq, hk, hlf, hv, hg, aq, ak, av, ga, gb = _inproj(
        x.reshape(m, d), norm1_g[0].reshape(1, d), w_in[0], hgrn_lb_logits, og, qg, kg,
        _block_diag_ones(ATT_WIDTH, ATT_HD), _block_diag_ones(ATT_KV_WIDTH, ATT_HD), tm)

    r3 = lambda a: a.reshape(b, s, a.shape[-1])
    ts = 2 * tm if s % (2 * tm) == 0 else tm
    o_a, o_b = _seq(attn_sinks[0], r3(hq), r3(hk), r3(hlf), r3(hv), r3(hg), r3(aq), r3(ak), r3(av), ts)

    return _post(x, o_a, o_b, r3(ga), r3(gb),
                 w_branch_a[0].astype(BF16), w_branch_b[0].astype(BF16), w_out[0].astype(BF16),
                 norm2_g[0].reshape(1, d), w_up[0].astype(BF16),
                 conv_w[0], conv_b[0].reshape(1, D_FF), w_down[0].astype(BF16), tm)
```

```python
import functools

import numpy as np
import jax
import jax.numpy as jnp
from jax import lax
from jax.experimental import pallas as pl
from jax.experimental.pallas import tpu as pltpu

F32 = jnp.float32
BF16 = jnp.bfloat16

D_MODEL = 1024
HG_HEADS = 4
HG_DK = 128
HG_DV = 128
HG_WIDTH = HG_HEADS * HG_DK
HG_CHUNK = 32
ATT_HEADS = 8
ATT_KV_HEADS = 2
ATT_GROUP = ATT_HEADS // ATT_KV_HEADS
ATT_HD = 64
ATT_WIDTH = ATT_HEADS * ATT_HD
ATT_KV_WIDTH = ATT_KV_HEADS * ATT_HD
WINDOW = 128
D_FF = 2816
CONV_W = 3
EPS = 1e-6
NEG = -1e30
LOG2E = 1.4426950408889634

LANES = 128
BF16_ROWS = 16
HG_TILE = 128
FF_CHUNK = 256
VMEM_LIMIT = 56 * 1024 * 1024

C_HQ, C_HF, C_HI, C_HG = 0, HG_WIDTH, 2 * HG_WIDTH, 3 * HG_WIDTH
C_AQ = 4 * HG_WIDTH
C_AKV = C_AQ + ATT_WIDTH
C_GA = C_AKV + 2 * ATT_KV_WIDTH
C_GB = C_GA + D_MODEL
IN_COLS = C_GB + D_MODEL


def _dot(a, b):
    return jnp.dot(a, b, preferred_element_type=F32)


def _dot_nt(a, b):
    return lax.dot_general(a, b, (((1,), (1,)), ((), ())), preferred_element_type=F32)


def _dot_tn(a, b):
    return lax.dot_general(a, b, (((0,), (0,)), ((), ())), preferred_element_type=F32)


def _rms(x, g):
    ms = jnp.mean(x * x, axis=-1, keepdims=True)
    return (x * lax.rsqrt(ms + EPS)) * g


def _inproj_kernel(x_ref, g1_ref, w_ref, lbl_ref, og_ref, qg_ref, kg_ref, bdq_ref, bdk_ref,
                   hq_ref, hk_ref, hlf_ref, hv_ref, hg_ref, aq_ref, ak_ref, av_ref, ga_ref, gb_ref, wb_ref):
    @pl.when(pl.program_id(0) == 0)
    def _():
        wb_ref[...] = w_ref[...].astype(BF16)

    h = _rms(x_ref[...], g1_ref[...]).astype(BF16)

    def proj(c0, width):
        return _dot(h, wb_ref[:, c0:c0 + width])

    lbl = lbl_ref[...]
    e = jnp.exp(lbl - jnp.max(lbl, axis=0, keepdims=True))
    lb = e[0:1] / jnp.sum(e, axis=0, keepdims=True)

    q_pre = proj(C_HQ, HG_WIDTH)
    hq_ref[...] = (q_pre * jax.nn.sigmoid(q_pre)) * (HG_DK ** -0.5)
    f = lb + (1.0 - lb) * jax.nn.sigmoid(proj(C_HF, HG_WIDTH))
    hk_ref[...] = 1.0 - f
    hlf_ref[...] = jnp.log2(f)
    g_pre = proj(C_HG, HG_WIDTH)
    hg_ref[...] = ((g_pre * jax.nn.sigmoid(g_pre)) * og_ref[...]).astype(BF16)

    def gate(dst, c_w, part):
        w = D_MODEL // 2
        dst[:, part * w:(part + 1) * w] = jax.nn.sigmoid(proj(c_w + part * w, w)).astype(BF16)

    assert C_AKV == C_AQ + ATT_WIDTH
    aqkv = proj(C_AQ, ATT_WIDTH + 2 * ATT_KV_WIDTH)
    a_q, kv = aqkv[:, :ATT_WIDTH], aqkv[:, ATT_WIDTH:]
    gate(ga_ref, C_GA, 0)
    sq = (a_q * a_q).astype(BF16)
    nbd = bdq_ref.shape[0]
    ms_q = jnp.concatenate([_dot(sq[:, c:c + nbd], bdq_ref[...]) for c in range(0, ATT_WIDTH, nbd)],
                           axis=1) * (1.0 / ATT_HD)
    aq_ref[...] = (((a_q * lax.rsqrt(ms_q + EPS)) * qg_ref[...]) * (LOG2E * ATT_HD ** -0.5)).astype(BF16)
    gate(ga_ref, C_GA, 1)
    a_k = kv[:, :ATT_KV_WIDTH]
    ms_k = _dot((a_k * a_k).astype(BF16), bdk_ref[...]) * (1.0 / ATT_HD)
    lo = lax.broadcasted_iota(jnp.int32, (1, LANES), 1) < ATT_HD
    for src, dst in (((a_k * lax.rsqrt(ms_k + EPS)) * kg_ref[...], ak_ref), (kv[:, ATT_KV_WIDTH:], av_ref)):
        swapped = pltpu.roll(src, ATT_HD, axis=1)
        dst[:, :LANES] = jnp.where(lo, src, swapped).astype(BF16)
        dst[:, LANES:] = jnp.where(lo, swapped, src).astype(BF16)
    gate(gb_ref, C_GB, 0)
    gate(gb_ref, C_GB, 1)
    hv_ref[...] = proj(C_HI, HG_WIDTH).astype(BF16)


def _inproj(x2, g1, w, lbl, og, qg, kg, bdq, bdk, tm):
    m = x2.shape[0]
    row = lambda c: pl.BlockSpec((tm, c), lambda i: (i, 0))
    full = lambda a: pl.BlockSpec(a.shape, lambda i: (0,) * a.ndim)
    outs = [
        (HG_WIDTH, F32), (HG_WIDTH, F32), (HG_WIDTH, F32), (HG_WIDTH, BF16), (HG_WIDTH, BF16),
        (ATT_WIDTH, BF16), (2 * LANES, BF16), (2 * LANES, BF16), (D_MODEL, BF16), (D_MODEL, BF16),
    ]
    return pl.pallas_call(
        _inproj_kernel,
        grid=(m // tm,),
        in_specs=[row(D_MODEL), full(g1),
                  pl.BlockSpec(w.shape, lambda i: (0, 0), pipeline_mode=pl.Buffered(1)),
                  full(lbl), full(og), full(qg), full(kg), full(bdq), full(bdk)],
        out_specs=[row(c) for c, _ in outs],
        out_shape=[jax.ShapeDtypeStruct((m, c), dt) for c, dt in outs],
        scratch_shapes=[pltpu.VMEM(w.shape, BF16)],
        compiler_params=pltpu.CompilerParams(dimension_semantics=("arbitrary",), vmem_limit_bytes=VMEM_LIMIT),
        name="inproj",
    )(x2, g1, w, lbl, og, qg, kg, bdq, bdk)


def _hgrn_head(q, k, v, g, sg, st, level):
    c = HG_CHUNK
    n = HG_TILE // c
    qs = [q[i * c:(i + 1) * c] for i in range(n)]
    ks = [k[i * c:(i + 1) * c] for i in range(n)]
    gs = [g[i * c:(i + 1) * c] for i in range(n)]
    mid = [gc[c // 2 - 1:c // 2] for gc in gs]
    end = [gc[c - 1:c] for gc in gs]
    p2 = jnp.exp2
    cat = lambda parts: jnp.concatenate([p.astype(BF16) for p in parts], axis=0)
    zero = jnp.zeros((c, HG_DK), BF16)

    s1 = _dot_nt(cat([qs[i] * p2(gs[i] - mid[i]) for i in range(n)]),
                 cat([ks[i] * p2(mid[i] - gs[i]) for i in range(n)]))
    k2_0 = ks[0] * p2(end[0] - gs[0])
    k2_2 = ks[2] * p2(end[2] - gs[2])
    q2_1 = qs[1] * p2(gs[1] - end[0])
    q2_3 = qs[3] * p2(gs[3] - end[2])
    s2 = _dot_nt(cat([zero, q2_1, zero, q2_3]), cat([k2_0, zero, k2_2, zero]))
    k3_0 = k2_0 * p2(end[1] - end[0])
    k3_1 = ks[1] * p2(end[1] - gs[1])
    q3_2 = qs[2] * p2(gs[2] - end[1])
    q3_3 = q2_3 * p2(end[2] - end[1])
    s3 = _dot_nt(cat([zero, zero, q3_2, q3_3]), cat([k3_0, k3_1, zero, zero]))
    scores = jnp.where(level == 1, s1, jnp.where(level == 2, s2, s3))
    o = _dot(scores.astype(BF16), v)

    d_half = p2(end[3] - end[1])
    k_in = cat([k3_0 * d_half, k3_1 * d_half, k2_2 * p2(end[3] - end[2]), ks[3] * p2(end[3] - gs[3])])
    q_in = cat([qs[0] * p2(gs[0]), q2_1 * p2(end[0]), q3_2 * p2(end[1]), q3_3 * p2(end[1])])
    o = o + _dot_nt(q_in, st.astype(BF16))
    st_new = st * p2(end[3]) + _dot_tn(v, k_in)

    ms = jnp.mean(o * o, axis=-1, keepdims=True)
    return (o * lax.rsqrt(ms + EPS)) * sg, st_new


def _hgrn_prefix(i, lf_ref, tri_ref):
    lf = lf_ref[i * HG_TILE:(i + 1) * HG_TILE, :]
    lf_hi = lf.astype(BF16)
    lf_lo = (lf - lf_hi.astype(F32)).astype(BF16)
    return _dot(tri_ref[...], jnp.concatenate([lf_hi, lf_lo], axis=0))


def _hgrn_step(i, h, g_all, q_ref, k_ref, v_ref, sg_ref, level, o_ref, st_ref):
    rs = slice(i * HG_TILE, (i + 1) * HG_TILE)
    cs = slice(h * HG_DK, (h + 1) * HG_DK)
    o, st_new = _hgrn_head(q_ref[rs, cs], k_ref[rs, cs], v_ref[rs, cs], g_all[:, cs],
                           sg_ref[rs, cs].astype(F32), st_ref[h], level)
    st_ref[h] = st_new
    o_ref[rs, cs] = o.astype(BF16)


def _swa_step(i, j, first_tile, sink_ref, q_ref, kc_ref, kp_ref, vc_ref, vp_ref, bias_ref, o_ref):
    w = WINDOW
    lane = lax.broadcasted_iota(jnp.int32, (1, LANES), 1)
    lo_b = (lane < ATT_HD).astype(BF16)
    hi_b = (lane >= ATT_HD).astype(BF16)
    lo = lax.broadcasted_iota(jnp.int32, (w, LANES), 1) < ATT_HD
    col0 = lax.broadcasted_iota(jnp.int32, (2 * w, LANES), 1) == 0
    upper = lax.broadcasted_iota(jnp.int32, (2 * w, LANES), 0) < w
    not_row0 = (lax.broadcasted_iota(jnp.int32, (BF16_ROWS, LANES), 0) > 0).astype(BF16)

    kvh = j // (ATT_GROUP // 2)
    cs = slice(kvh * LANES, (kvh + 1) * LANES)
    if i == 0:
        kk = jnp.concatenate([kp_ref[:, cs], kc_ref[0:w, cs]], axis=0)
        vv = jnp.concatenate([vp_ref[0:BF16_ROWS, cs] * not_row0, vp_ref[BF16_ROWS:, cs], vc_ref[0:w, cs]], axis=0)
    else:
        r0 = (i - 1) * w
        kk = kc_ref[r0:r0 + 2 * w, cs]
        vv = jnp.concatenate([vc_ref[r0:r0 + BF16_ROWS, cs] * not_row0,
                              vc_ref[r0 + BF16_ROWS:r0 + 2 * w, cs]], axis=0)
    qc = q_ref[i * w:(i + 1) * w, j * LANES:(j + 1) * LANES]
    qs = jnp.concatenate([qc * lo_b, qc * hi_b], axis=0)
    s = _dot_nt(qs, kk) + bias_ref[j]
    s_prev, s_cur = s[:, :w], s[:, w:]
    if i == 0:
        s_prev = s_prev + jnp.where(first_tile, NEG, 0.0)
    sink = jnp.where(upper, sink_ref[2 * j] * LOG2E, sink_ref[2 * j + 1] * LOG2E)
    s_prev = jnp.where(col0, sink, s_prev)
    m = jnp.max(jnp.maximum(s_prev, s_cur), axis=-1, keepdims=True)
    e_prev = jnp.exp2(s_prev - m)
    e_cur = jnp.exp2(s_cur - m)
    den = jnp.sum(e_prev + e_cur, axis=-1, keepdims=True)
    e = jnp.concatenate([e_prev.astype(BF16), e_cur.astype(BF16)], axis=1)
    o2 = _dot(e, vv) * (1.0 / den)
    o_ref[i * w:(i + 1) * w, j * LANES:(j + 1) * LANES] = jnp.where(lo, o2[:w], o2[w:]).astype(BF16)


def _seq_kernel(sink_ref, hq_ref, hk_ref, hlf_ref, hv_ref, hg_ref, tri_ref, lvl_ref,
                aq_ref, kc_ref, kp_ref, vc_ref, vp_ref, bias_ref, oa_ref, ob_ref, st_ref, *, nsub):
    first_tile = pl.program_id(1) == 0

    @pl.when(first_tile)
    def _():
        st_ref[...] = jnp.zeros_like(st_ref)

    level = lvl_ref[...]
    assert HG_HEADS == ATT_HEADS // 2 and HG_TILE == WINDOW
    for i in range(nsub):
        g_all = _hgrn_prefix(i, hlf_ref, tri_ref)
        for h in range(HG_HEADS):
            _hgrn_step(i, h, g_all, hq_ref, hk_ref, hv_ref, hg_ref, level, oa_ref, st_ref)
            _swa_step(i, h, first_tile, sink_ref, aq_ref, kc_ref, kp_ref, vc_ref, vp_ref, bias_ref, ob_ref)


def _hgrn_consts():
    t = HG_TILE
    row = np.arange(t)[:, None]
    col = np.arange(t)[None, :]
    tri = (col <= row).astype(np.float32)
    ci, cj = row // HG_CHUNK, col // HG_CHUNK
    level = np.where((ci == cj) & (col <= row), 1, np.where((ci == cj + 1) & (cj % 2 == 0), 2, 0))
    return jnp.asarray(np.concatenate([tri, tri], axis=1), dtype=BF16), jnp.asarray(level, dtype=jnp.int32)


def _swa_bias():
    w = WINDOW
    i = np.arange(w)[:, None]
    j = np.arange(2 * w)[None, :]
    delta = (i + w - j).astype(np.float32)
    band = (delta >= 0) & (delta < w)
    assert not band[:, 0].any()
    slopes = 2.0 ** (-8.0 * np.arange(1, ATT_HEADS + 1, dtype=np.float32) / ATT_HEADS)
    per_head = np.where(band[None], -slopes[:, None, None] * delta[None] * np.float32(LOG2E), np.float32(NEG))
    return jnp.asarray(per_head.astype(np.float32).reshape(ATT_HEADS // 2, 2 * w, 2 * w))


def _seq(sinks, hq, hk, hlf, hv, hg, aq, ak, av, ts):
    b, s, _ = hq.shape
    tri, level = _hgrn_consts()
    bias = _swa_bias()
    nprev = ts // WINDOW
    wide = pl.BlockSpec((None, ts, HG_WIDTH), lambda bi, si, *_: (bi, si, 0))
    cur = pl.BlockSpec((None, ts, 2 * LANES), lambda bi, si, *_: (bi, si, 0))
    prev = pl.BlockSpec((None, WINDOW, 2 * LANES), lambda bi, si, *_: (bi, jnp.maximum(si * nprev - 1, 0), 0))
    full = lambda a: pl.BlockSpec(a.shape, lambda bi, si, *_: (0,) * a.ndim)
    return pl.pallas_call(
        functools.partial(_seq_kernel, nsub=ts // HG_TILE),
        grid_spec=pltpu.PrefetchScalarGridSpec(
            num_scalar_prefetch=1,
            grid=(b, s // ts),
            in_specs=[wide, wide, wide, wide, wide, full(tri), full(level),
                      wide, cur, prev, cur, prev, full(bias)],
            out_specs=[wide, wide],
            scratch_shapes=[pltpu.VMEM((HG_HEADS, HG_DV, HG_DK), F32)],
        ),
        out_shape=[jax.ShapeDtypeStruct((b, s, HG_WIDTH), BF16), jax.ShapeDtypeStruct((b, s, ATT_WIDTH), BF16)],
        compiler_params=pltpu.CompilerParams(dimension_semantics=("arbitrary", "arbitrary"),
                                             vmem_limit_bytes=VMEM_LIMIT),
        name="seq",
    )(sinks, hq, hk, hlf, hv, hg, tri, level, aq, ak, ak, av, av, bias)


def _post_kernel(x_ref, oa_ref, ob_ref, ga_ref, gb_ref, wa_ref, wb_ref, wo_ref, g2_ref,
                 wu_ref, cw_ref, cb_ref, wd_ref, o_ref, x1_ref, h2_ref, gs_ref, carry_ref, act_ref, *, tm):
    pad = 8

    @pl.when(pl.program_id(1) == 0)
    def _():
        carry_ref[...] = jnp.zeros_like(carry_ref)

    mixed = (ga_ref[...].astype(F32) * _dot(oa_ref[...], wa_ref[...])
             + gb_ref[...].astype(F32) * _dot(ob_ref[...], wb_ref[...]))
    x1_ref[...] = x_ref[...] + _dot(mixed.astype(BF16), wo_ref[...])
    h2_ref[...] = _rms(x1_ref[...], g2_ref[...]).astype(BF16)

    h2 = h2_ref[...]
    for c0 in range(0, D_FF, FF_CHUNK):
        cw = min(FF_CHUNK, D_FF - c0)
        cs = slice(c0, c0 + cw)
        gate = _dot(h2, wu_ref[:, cs])
        val = _dot(h2, wu_ref[:, D_FF + c0:D_FF + c0 + cw])
        gs_ref[0:pad, :cw] = carry_ref[:, cs]
        gs_ref[pad:pad + tm, :cw] = gate
        carry_ref[:, cs] = gate[tm - pad:tm, :]
        conv = cb_ref[:, cs] + cw_ref[2:3, cs] * gate
        conv = conv + cw_ref[1:2, cs] * gs_ref[pad - 1:pad - 1 + tm, :cw]
        conv = conv + cw_ref[0:1, cs] * gs_ref[pad - 2:pad - 2 + tm, :cw]
        gelu = (0.5 * conv) * (1.0 + lax.erf(conv * (0.5 ** 0.5)))
        act_ref[:, cs] = (gelu * val).astype(BF16)
    o_ref[...] = x1_ref[...] + _dot(act_ref[...], wd_ref[...])


def _post(x, oa, ob, ga, gb, wa, wb, wo, g2, wu, cw, cb, wd, tm):
    b, s, _ = x.shape
    row = lambda c: pl.BlockSpec((None, tm, c), lambda bi, si: (bi, si, 0))
    full = lambda a: pl.BlockSpec(a.shape, lambda bi, si: (0,) * a.ndim, pipeline_mode=pl.Buffered(1))
    return pl.pallas_call(
        functools.partial(_post_kernel, tm=tm),
        grid=(b, s // tm),
        in_specs=[row(D_MODEL), row(HG_WIDTH), row(ATT_WIDTH), row(D_MODEL), row(D_MODEL),
                  full(wa), full(wb), full(wo), full(g2), full(wu), full(cw), full(cb), full(wd)],
        out_specs=row(D_MODEL),
        out_shape=jax.ShapeDtypeStruct((b, s, D_MODEL), F32),
        scratch_shapes=[pltpu.VMEM((tm, D_MODEL), F32),
                        pltpu.VMEM((tm, D_MODEL), BF16),
                        pltpu.VMEM((tm + 8, FF_CHUNK), F32),
                        pltpu.VMEM((8, D_FF), F32),
                        pltpu.VMEM((tm, D_FF), BF16)],
        compiler_params=pltpu.CompilerParams(dimension_semantics=("arbitrary", "arbitrary"),
                                             vmem_limit_bytes=VMEM_LIMIT),
        name="post",
    )(x, oa, ob, ga, gb, wa, wb, wo, g2, wu, cw, cb, wd)


def _block_diag_ones(n, blk):
    idx = np.arange(n) // blk
    return jnp.asarray((idx[:, None] == idx[None, :]).astype(np.float32), dtype=BF16)


def kernel(x, norm1_g, w_in, hgrn_lb_logits, hgrn_out_g, q_norm_g, k_norm_g, attn_sinks,
           w_branch_a, w_branch_b, w_out, norm2_g, w_up, conv_w, conv_b, w_down):
    b, s, d = x.shape
    assert d == D_MODEL and w_in.shape == (1, D_MODEL, IN_COLS) and s % 512 == 0
    m = b * s
    tm = 512

    og = jnp.tile(hgrn_out_g[0], HG_HEADS).reshape(1, HG_WIDTH)
    qg = jnp.tile(q_norm_g[0], ATT_HEADS).reshape(1, ATT_WIDTH)
    kg = jnp.tile(k_norm_g[0], ATT_KV_HEADS).reshape(1, ATT_KV_WIDTH)
    hq, hk, hlf, hv, hg, aq, ak, av, ga, gb = _inproj(
        x.reshape(m, d), norm1_g[0].reshape(1, d), w_in[0], hgrn_lb_logits, og, qg, kg,
        _block_diag_ones(2 * LANES, ATT_HD), _block_diag_ones(ATT_KV_WIDTH, ATT_HD), tm)

    r3 = lambda a: a.reshape(b, s, a.shape[-1])
    ts = 2 * tm if s % (2 * tm) == 0 else tm
    o_a, o_b = _seq(attn_sinks[0], r3(hq), r3(hk), r3(hlf), r3(hv), r3(hg), r3(aq), r3(ak), r3(av), ts)

    return _post(x, o_a, o_b, r3(ga), r3(gb),
                 w_branch_a[0].astype(BF16), w_branch_b[0].astype(BF16), w_out[0].astype(BF16),
                 norm2_g[0].reshape(1, d), w_up[0].astype(BF16),
                 conv_w[0], conv_b[0].reshape(1, D_FF), w_down[0].astype(BF16), tm)
```

```python
import functools

import numpy as np
import jax
import jax.numpy as jnp
from jax import lax
from jax.experimental import pallas as pl
from jax.experimental.pallas import tpu as pltpu

F32 = jnp.float32
BF16 = jnp.bfloat16

D_MODEL = 1024
HG_HEADS = 4
HG_DK = 128
HG_DV = 128
HG_WIDTH = HG_HEADS * HG_DK
HG_CHUNK = 32
ATT_HEADS = 8
ATT_KV_HEADS = 2
ATT_GROUP = ATT_HEADS // ATT_KV_HEADS
ATT_HD = 64
ATT_WIDTH = ATT_HEADS * ATT_HD
ATT_KV_WIDTH = ATT_KV_HEADS * ATT_HD
WINDOW = 128
D_FF = 2816
CONV_W = 3
EPS = 1e-6
NEG = -1e30
LOG2E = 1.4426950408889634

LANES = 128
BF16_ROWS = 16
HG_TILE = 128
FF_CHUNK = 256
VMEM_LIMIT = 56 * 1024 * 1024

C_HQ, C_HF, C_HI, C_HG = 0, HG_WIDTH, 2 * HG_WIDTH, 3 * HG_WIDTH
C_AQ = 4 * HG_WIDTH
C_AKV = C_AQ + ATT_WIDTH
C_GA = C_AKV + 2 * ATT_KV_WIDTH
C_GB = C_GA + D_MODEL
IN_COLS = C_GB + D_MODEL


def _dot(a, b):
    return jnp.dot(a, b, preferred_element_type=F32)


def _dot_nt(a, b):
    return lax.dot_general(a, b, (((1,), (1,)), ((), ())), preferred_element_type=F32)


def _dot_tn(a, b):
    return lax.dot_general(a, b, (((0,), (0,)), ((), ())), preferred_element_type=F32)


def _rms(x, g):
    ms = jnp.mean(x * x, axis=-1, keepdims=True)
    return (x * lax.rsqrt(ms + EPS)) * g


def _inproj_kernel(x_ref, g1_ref, w_ref, lbl_ref, og_ref, qg_ref, kg_ref, bdq_ref, bdk_ref,
                   hq_ref, hk_ref, hlf_ref, hv_ref, hg_ref, aq_ref, ak_ref, av_ref, ga_ref, gb_ref, wb_ref):
    @pl.when(pl.program_id(0) == 0)
    def _():
        wb_ref[...] = w_ref[...].astype(BF16)

    h = _rms(x_ref[...], g1_ref[...]).astype(BF16)

    def proj(c0, width):
        return _dot(h, wb_ref[:, c0:c0 + width])

    lbl = lbl_ref[...]
    e = jnp.exp(lbl - jnp.max(lbl, axis=0, keepdims=True))
    lb = e[0:1] / jnp.sum(e, axis=0, keepdims=True)

    q_pre = proj(C_HQ, HG_WIDTH)
    hq_ref[...] = (q_pre * jax.nn.sigmoid(q_pre)) * (HG_DK ** -0.5)
    f = lb + (1.0 - lb) * jax.nn.sigmoid(proj(C_HF, HG_WIDTH))
    hk_ref[...] = 1.0 - f
    hlf_ref[...] = jnp.log2(f)
    g_pre = proj(C_HG, HG_WIDTH)
    hg_ref[...] = ((g_pre * jax.nn.sigmoid(g_pre)) * og_ref[...]).astype(BF16)

    def gate(dst, c_w, part):
        w = D_MODEL // 2
        dst[:, part * w:(part + 1) * w] = jax.nn.sigmoid(proj(c_w + part * w, w)).astype(BF16)

    assert C_AKV == C_AQ + ATT_WIDTH
    aqkv = proj(C_AQ, ATT_WIDTH + 2 * ATT_KV_WIDTH)
    a_q, kv = aqkv[:, :ATT_WIDTH], aqkv[:, ATT_WIDTH:]
    gate(ga_ref, C_GA, 0)
    sq = (a_q * a_q).astype(BF16)
    nbd = bdq_ref.shape[0]
    ms_q = jnp.concatenate([_dot(sq[:, c:c + nbd], bdq_ref[...]) for c in range(0, ATT_WIDTH, nbd)],
                           axis=1) * (1.0 / ATT_HD)
    aq_ref[...] = (((a_q * lax.rsqrt(ms_q + EPS)) * qg_ref[...]) * (LOG2E * ATT_HD ** -0.5)).astype(BF16)
    gate(ga_ref, C_GA, 1)
    a_k = kv[:, :ATT_KV_WIDTH]
    ms_k = _dot((a_k * a_k).astype(BF16), bdk_ref[...]) * (1.0 / ATT_HD)
    lo = lax.broadcasted_iota(jnp.int32, (1, LANES), 1) < ATT_HD
    for src, dst in (((a_k * lax.rsqrt(ms_k + EPS)) * kg_ref[...], ak_ref), (kv[:, ATT_KV_WIDTH:], av_ref)):
        swapped = pltpu.roll(src, ATT_HD, axis=1)
        dst[:, :LANES] = jnp.where(lo, src, swapped).astype(BF16)
        dst[:, LANES:] = jnp.where(lo, swapped, src).astype(BF16)
    gate(gb_ref, C_GB, 0)
    gate(gb_ref, C_GB, 1)
    hv_ref[...] = proj(C_HI, HG_WIDTH).astype(BF16)


def _inproj(x2, g1, w, lbl, og, qg, kg, bdq, bdk, tm):
    m = x2.shape[0]
    row = lambda c: pl.BlockSpec((tm, c), lambda i: (i, 0))
    full = lambda a: pl.BlockSpec(a.shape, lambda i: (0,) * a.ndim)
    outs = [
        (HG_WIDTH, F32), (HG_WIDTH, F32), (HG_WIDTH, F32), (HG_WIDTH, BF16), (HG_WIDTH, BF16),
        (ATT_WIDTH, BF16), (2 * LANES, BF16), (2 * LANES, BF16), (D_MODEL, BF16), (D_MODEL, BF16),
    ]
    return pl.pallas_call(
        _inproj_kernel,
        grid=(m // tm,),
        in_specs=[row(D_MODEL), full(g1),
                  pl.BlockSpec(w.shape, lambda i: (0, 0), pipeline_mode=pl.Buffered(1)),
                  full(lbl), full(og), full(qg), full(kg), full(bdq), full(bdk)],
        out_specs=[row(c) for c, _ in outs],
        out_shape=[jax.ShapeDtypeStruct((m, c), dt) for c, dt in outs],
        scratch_shapes=[pltpu.VMEM(w.shape, BF16)],
        compiler_params=pltpu.CompilerParams(dimension_semantics=("arbitrary",), vmem_limit_bytes=VMEM_LIMIT),
        name="inproj",
    )(x2, g1, w, lbl, og, qg, kg, bdq, bdk)


def _hgrn_head(q, k, v, g, sg, st, level):
    c = HG_CHUNK
    n = HG_TILE // c
    qs = [q[i * c:(i + 1) * c].astype(BF16) for i in range(n)]
    ks = [k[i * c:(i + 1) * c].astype(BF16) for i in range(n)]
    gs = [g[i * c:(i + 1) * c] for i in range(n)]
    mid = [gc[c // 2 - 1:c // 2] for gc in gs]
    end = [gc[c - 1:c] for gc in gs]
    p2 = lambda x: jnp.exp2(x).astype(BF16)
    cat = lambda parts: jnp.concatenate(parts, axis=0)
    zero = jnp.zeros((c, HG_DK), BF16)

    s1 = _dot_nt(cat([qs[i] * p2(gs[i] - mid[i]) for i in range(n)]),
                 cat([ks[i] * p2(mid[i] - gs[i]) for i in range(n)]))
    k2_0 = ks[0] * p2(end[0] - gs[0])
    k2_2 = ks[2] * p2(end[2] - gs[2])
    q2_1 = qs[1] * p2(gs[1] - end[0])
    q2_3 = qs[3] * p2(gs[3] - end[2])
    s2 = _dot_nt(cat([zero, q2_1, zero, q2_3]), cat([k2_0, zero, k2_2, zero]))
    k3_0 = k2_0 * p2(end[1] - end[0])
    k3_1 = ks[1] * p2(end[1] - gs[1])
    q3_2 = qs[2] * p2(gs[2] - end[1])
    q3_3 = q2_3 * p2(end[2] - end[1])
    s3 = _dot_nt(cat([zero, zero, q3_2, q3_3]), cat([k3_0, k3_1, zero, zero]))
    scores = jnp.where(level == 1, s1, jnp.where(level == 2, s2, s3))
    o = _dot(scores.astype(BF16), v)

    d_half = p2(end[3] - end[1])
    k_in = cat([k3_0 * d_half, k3_1 * d_half, k2_2 * p2(end[3] - end[2]), ks[3] * p2(end[3] - gs[3])])
    q_in = cat([qs[0] * p2(gs[0]), q2_1 * p2(end[0]), q3_2 * p2(end[1]), q3_3 * p2(end[1])])
    o = o + _dot_nt(q_in, st.astype(BF16))
    st_new = st * jnp.exp2(end[3]) + _dot_tn(v, k_in)

    ms = jnp.mean(o * o, axis=-1, keepdims=True)
    return (o * lax.rsqrt(ms + EPS)) * sg, st_new


def _hgrn_prefix(i, lf_ref, tri_ref):
    lf = lf_ref[i * HG_TILE:(i + 1) * HG_TILE, :]
    lf_hi = lf.astype(BF16)
    lf_lo = (lf - lf_hi.astype(F32)).astype(BF16)
    return _dot(tri_ref[...], jnp.concatenate([lf_hi, lf_lo], axis=0))


def _hgrn_step(i, h, g_all, q_ref, k_ref, v_ref, sg_ref, level, o_ref, st_ref):
    rs = slice(i * HG_TILE, (i + 1) * HG_TILE)
    cs = slice(h * HG_DK, (h + 1) * HG_DK)
    o, st_new = _hgrn_head(q_ref[rs, cs], k_ref[rs, cs], v_ref[rs, cs], g_all[:, cs],
                           sg_ref[rs, cs].astype(F32), st_ref[h], level)
    st_ref[h] = st_new
    o_ref[rs, cs] = o.astype(BF16)


def _swa_step(i, j, first_tile, sink_ref, q_ref, kc_ref, kp_ref, vc_ref, vp_ref, bias_ref, o_ref):
    w = WINDOW
    lane = lax.broadcasted_iota(jnp.int32, (1, LANES), 1)
    lo_b = (lane < ATT_HD).astype(BF16)
    hi_b = (lane >= ATT_HD).astype(BF16)
    lo = lax.broadcasted_iota(jnp.int32, (w, LANES), 1) < ATT_HD
    row0 = lax.broadcasted_iota(jnp.int32, (8, LANES), 0) == 0
    not_row0 = (lax.broadcasted_iota(jnp.int32, (BF16_ROWS, LANES), 0) > 0).astype(BF16)

    kvh = j // (ATT_GROUP // 2)
    cs = slice(kvh * LANES, (kvh + 1) * LANES)
    if i == 0:
        kk = jnp.concatenate([kp_ref[:, cs], kc_ref[0:w, cs]], axis=0)
        vv = jnp.concatenate([vp_ref[0:BF16_ROWS, cs] * not_row0, vp_ref[BF16_ROWS:, cs], vc_ref[0:w, cs]], axis=0)
    else:
        r0 = (i - 1) * w
        kk = kc_ref[r0:r0 + 2 * w, cs]
        vv = jnp.concatenate([vc_ref[r0:r0 + BF16_ROWS, cs] * not_row0,
                              vc_ref[r0 + BF16_ROWS:r0 + 2 * w, cs]], axis=0)
    qc = q_ref[i * w:(i + 1) * w, j * LANES:(j + 1) * LANES]
    qs = jnp.concatenate([qc * lo_b, qc * hi_b], axis=0)
    st = _dot_nt(kk, qs) + bias_ref[j]
    if i == 0:
        st = jnp.concatenate([st[:w] + jnp.where(first_tile, NEG, 0.0), st[w:]], axis=0)
    sink = jnp.concatenate([jnp.broadcast_to(sink_ref[2 * j + n] * LOG2E, (8, LANES)) for n in range(2)], axis=1)
    st = jnp.concatenate([jnp.where(jnp.concatenate([row0, row0], axis=1), sink, st[:8]), st[8:]], axis=0)
    m = jnp.max(st, axis=0, keepdims=True)
    e = jnp.exp2(st - m)
    r = 1.0 / jnp.sum(e, axis=0, keepdims=True)
    o2 = _dot_tn(e.astype(BF16) * r.astype(BF16), vv)
    o_ref[i * w:(i + 1) * w, j * LANES:(j + 1) * LANES] = jnp.where(lo, o2[:w], o2[w:]).astype(BF16)


def _seq_kernel(sink_ref, hq_ref, hk_ref, hlf_ref, hv_ref, hg_ref, tri_ref, lvl_ref,
                aq_ref, kc_ref, kp_ref, vc_ref, vp_ref, bias_ref, oa_ref, ob_ref, st_ref, *, nsub):
    first_tile = pl.program_id(1) == 0

    @pl.when(first_tile)
    def _():
        st_ref[...] = jnp.zeros_like(st_ref)

    level = lvl_ref[...]
    assert HG_HEADS == ATT_HEADS // 2 and HG_TILE == WINDOW
    for i in range(nsub):
        g_all = _hgrn_prefix(i, hlf_ref, tri_ref)
        for h in range(HG_HEADS):
            _hgrn_step(i, h, g_all, hq_ref, hk_ref, hv_ref, hg_ref, level, oa_ref, st_ref)
            _swa_step(i, h, first_tile, sink_ref, aq_ref, kc_ref, kp_ref, vc_ref, vp_ref, bias_ref, ob_ref)


def _hgrn_consts():
    t = HG_TILE
    row = np.arange(t)[:, None]
    col = np.arange(t)[None, :]
    tri = (col <= row).astype(np.float32)
    ci, cj = row // HG_CHUNK, col // HG_CHUNK
    level = np.where((ci == cj) & (col <= row), 1, np.where((ci == cj + 1) & (cj % 2 == 0), 2, 0))
    return jnp.asarray(np.concatenate([tri, tri], axis=1), dtype=BF16), jnp.asarray(level, dtype=jnp.int32)


def _swa_bias():
    w = WINDOW
    i = np.arange(w)[:, None]
    j = np.arange(2 * w)[None, :]
    delta = (i + w - j).astype(np.float32)
    band = (delta >= 0) & (delta < w)
    assert not band[:, 0].any()
    slopes = 2.0 ** (-8.0 * np.arange(1, ATT_HEADS + 1, dtype=np.float32) / ATT_HEADS)
    per_head = np.where(band[None], -slopes[:, None, None] * delta[None] * np.float32(LOG2E), np.float32(NEG))
    pairs = per_head.astype(np.float32).reshape(ATT_HEADS // 2, 2 * w, 2 * w)
    return jnp.asarray(np.ascontiguousarray(pairs.transpose(0, 2, 1)))


def _seq(sinks, hq, hk, hlf, hv, hg, aq, ak, av, ts):
    b, s, _ = hq.shape
    tri, level = _hgrn_consts()
    bias = _swa_bias()
    nprev = ts // WINDOW
    wide = pl.BlockSpec((None, ts, HG_WIDTH), lambda bi, si, *_: (bi, si, 0))
    cur = pl.BlockSpec((None, ts, 2 * LANES), lambda bi, si, *_: (bi, si, 0))
    prev = pl.BlockSpec((None, WINDOW, 2 * LANES), lambda bi, si, *_: (bi, jnp.maximum(si * nprev - 1, 0), 0))
    full = lambda a: pl.BlockSpec(a.shape, lambda bi, si, *_: (0,) * a.ndim)
    return pl.pallas_call(
        functools.partial(_seq_kernel, nsub=ts // HG_TILE),
        grid_spec=pltpu.PrefetchScalarGridSpec(
            num_scalar_prefetch=1,
            grid=(b, s // ts),
            in_specs=[wide, wide, wide, wide, wide, full(tri), full(level),
                      wide, cur, prev, cur, prev, full(bias)],
            out_specs=[wide, wide],
            scratch_shapes=[pltpu.VMEM((HG_HEADS, HG_DV, HG_DK), F32)],
        ),
        out_shape=[jax.ShapeDtypeStruct((b, s, HG_WIDTH), BF16), jax.ShapeDtypeStruct((b, s, ATT_WIDTH), BF16)],
        compiler_params=pltpu.CompilerParams(dimension_semantics=("arbitrary", "arbitrary"),
                                             vmem_limit_bytes=VMEM_LIMIT),
        name="seq",
    )(sinks, hq, hk, hlf, hv, hg, tri, level, aq, ak, ak, av, av, bias)


def _post_kernel(x_ref, oa_ref, ob_ref, ga_ref, gb_ref, wa_ref, wb_ref, wo_ref, g2_ref,
                 wu_ref, cw_ref, cb_ref, wd_ref, o_ref, x1_ref, h2_ref, gs_ref, carry_ref, act_ref, *, tm):
    pad = 8

    @pl.when(pl.program_id(1) == 0)
    def _():
        carry_ref[...] = jnp.zeros_like(carry_ref)

    mixed = (ga_ref[...].astype(F32) * _dot(oa_ref[...], wa_ref[...])
             + gb_ref[...].astype(F32) * _dot(ob_ref[...], wb_ref[...]))
    x1_ref[...] = x_ref[...] + _dot(mixed.astype(BF16), wo_ref[...])
    h2_ref[...] = _rms(x1_ref[...], g2_ref[...]).astype(BF16)

    h2 = h2_ref[...]
    for c0 in range(0, D_FF, FF_CHUNK):
        cw = min(FF_CHUNK, D_FF - c0)
        cs = slice(c0, c0 + cw)
        gate = _dot(h2, wu_ref[:, cs])
        val = _dot(h2, wu_ref[:, D_FF + c0:D_FF + c0 + cw])
        gs_ref[0:pad, :cw] = carry_ref[:, cs]
        gs_ref[pad:pad + tm, :cw] = gate
        carry_ref[:, cs] = gate[tm - pad:tm, :]
        conv = cb_ref[:, cs] + cw_ref[2:3, cs] * gate
        conv = conv + cw_ref[1:2, cs] * gs_ref[pad - 1:pad - 1 + tm, :cw]
        conv = conv + cw_ref[0:1, cs] * gs_ref[pad - 2:pad - 2 + tm, :cw]
        gelu = (0.5 * conv) * (1.0 + lax.erf(conv * (0.5 ** 0.5)))
        act_ref[:, cs] = (gelu * val).astype(BF16)
    o_ref[...] = x1_ref[...] + _dot(act_ref[...], wd_ref[...])


def _post(x, oa, ob, ga, gb, wa, wb, wo, g2, wu, cw, cb, wd, tm):
    b, s, _ = x.shape
    row = lambda c: pl.BlockSpec((None, tm, c), lambda bi, si: (bi, si, 0))
    full = lambda a: pl.BlockSpec(a.shape, lambda bi, si: (0,) * a.ndim, pipeline_mode=pl.Buffered(1))
    return pl.pallas_call(
        functools.partial(_post_kernel, tm=tm),
        grid=(b, s // tm),
        in_specs=[row(D_MODEL), row(HG_WIDTH), row(ATT_WIDTH), row(D_MODEL), row(D_MODEL),
                  full(wa), full(wb), full(wo), full(g2), full(wu), full(cw), full(cb), full(wd)],
        out_specs=row(D_MODEL),
        out_shape=jax.ShapeDtypeStruct((b, s, D_MODEL), F32),
        scratch_shapes=[pltpu.VMEM((tm, D_MODEL), F32),
                        pltpu.VMEM((tm, D_MODEL), BF16),
                        pltpu.VMEM((tm + 8, FF_CHUNK), F32),
                        pltpu.VMEM((8, D_FF), F32),
                        pltpu.VMEM((tm, D_FF), BF16)],
        compiler_params=pltpu.CompilerParams(dimension_semantics=("arbitrary", "arbitrary"),
                                             vmem_limit_bytes=VMEM_LIMIT),
        name="post",
    )(x, oa, ob, ga, gb, wa, wb, wo, g2, wu, cw, cb, wd)


def _block_diag_ones(n, blk):
    idx = np.arange(n) // blk
    return jnp.asarray((idx[:, None] == idx[None, :]).astype(np.float32), dtype=BF16)


def kernel(x, norm1_g, w_in, hgrn_lb_logits, hgrn_out_g, q_norm_g, k_norm_g, attn_sinks,
           w_branch_a, w_branch_b, w_out, norm2_g, w_up, conv_w, conv_b, w_down):
    b, s, d = x.shape
    assert d == D_MODEL and w_in.shape == (1, D_MODEL, IN_COLS) and s % 512 == 0
    m = b * s
    tm = 512

    og = jnp.tile(hgrn_out_g[0], HG_HEADS).reshape(1, HG_WIDTH)
    qg = jnp.tile(q_norm_g[0], ATT_HEADS).reshape(1, ATT_WIDTH)
    kg = jnp.tile(k_norm_g[0], ATT_KV_HEADS).reshape(1, ATT_KV_WIDTH)
    hq, hk, hlf, hv, hg, aq, ak, av, ga, gb = _inproj(
        x.reshape(m, d), norm1_g[0].reshape(1, d), w_in[0], hgrn_lb_logits, og, qg, kg,
        _block_diag_ones(2 * LANES, ATT_HD), _block_diag_ones(ATT_KV_WIDTH, ATT_HD), tm)

    r3 = lambda a: a.reshape(b, s, a.shape[-1])
    ts = 2 * tm if s % (2 * tm) == 0 else tm
    o_a, o_b = _seq(attn_sinks[0], r3(hq), r3(hk), r3(hlf), r3(hv), r3(hg), r3(aq), r3(ak), r3(av), ts)

    return _post(x, o_a, o_b, r3(ga), r3(gb),
                 w_branch_a[0].astype(BF16), w_branch_b[0].astype(BF16), w_out[0].astype(BF16),
                 norm2_g[0].reshape(1, d), w_up[0].astype(BF16),
                 conv_w[0], conv_b[0].reshape(1, D_FF), w_down[0].astype(BF16), tm)
```

```python
import functools

import numpy as np
import jax
import jax.numpy as jnp
from jax import lax
from jax.experimental import pallas as pl
from jax.experimental.pallas import tpu as pltpu

F32 = jnp.float32
BF16 = jnp.bfloat16

D_MODEL = 1024
HG_HEADS = 4
HG_DK = 128
HG_DV = 128
HG_WIDTH = HG_HEADS * HG_DK
HG_CHUNK = 32
ATT_HEADS = 8
ATT_KV_HEADS = 2
ATT_GROUP = ATT_HEADS // ATT_KV_HEADS
ATT_HD = 64
ATT_WIDTH = ATT_HEADS * ATT_HD
ATT_KV_WIDTH = ATT_KV_HEADS * ATT_HD
WINDOW = 128
D_FF = 2816
CONV_W = 3
EPS = 1e-6
NEG = -1e30
LOG2E = 1.4426950408889634

LANES = 128
SUBLANES = 8
BF16_ROWS = 2 * SUBLANES
V7X_VMEM_BYTES = 64 * 1024 * 1024
VMEM_LIMIT = V7X_VMEM_BYTES * 7 // 8
HG_TILE = 128
FF_CHUNK = 256
ROW_TILE = 512


def _tiles(s):
    assert s % ROW_TILE == 0 and ROW_TILE % HG_TILE == 0
    return ROW_TILE, 2 * ROW_TILE if s % (2 * ROW_TILE) == 0 else ROW_TILE

C_HQ, C_HF, C_HI, C_HG = 0, HG_WIDTH, 2 * HG_WIDTH, 3 * HG_WIDTH
C_AQ = 4 * HG_WIDTH
C_AKV = C_AQ + ATT_WIDTH
C_GA = C_AKV + 2 * ATT_KV_WIDTH
C_GB = C_GA + D_MODEL
IN_COLS = C_GB + D_MODEL


def _dot(a, b):
    return jnp.dot(a, b, preferred_element_type=F32)


def _dot_nt(a, b):
    return lax.dot_general(a, b, (((1,), (1,)), ((), ())), preferred_element_type=F32)


def _dot_tn(a, b):
    return lax.dot_general(a, b, (((0,), (0,)), ((), ())), preferred_element_type=F32)


def _sigmoid(x):
    return 0.5 * jnp.tanh(0.5 * x) + 0.5


def _rms(x, g):
    ms = jnp.mean(x * x, axis=-1, keepdims=True)
    return (x * lax.rsqrt(ms + EPS)) * g


def _inproj_kernel(x_ref, g1_ref, w_ref, lbl_ref, og_ref, qg_ref, kg_ref, bdq_ref, bdk_ref,
                   hq_ref, hk_ref, hlf_ref, hv_ref, hg_ref, aq_ref, ak_ref, av_ref, ga_ref, gb_ref, wb_ref):
    @pl.when(pl.program_id(0) == 0)
    def _():
        wb_ref[...] = w_ref[...].astype(BF16)

    h = _rms(x_ref[...], g1_ref[...]).astype(BF16)

    def proj(c0, width):
        return _dot(h, wb_ref[:, c0:c0 + width])

    lbl = lbl_ref[...]
    e = jnp.exp(lbl - jnp.max(lbl, axis=0, keepdims=True))
    lb = e[0:1] / jnp.sum(e, axis=0, keepdims=True)

    q_pre = proj(C_HQ, HG_WIDTH)
    hq_ref[...] = (q_pre * _sigmoid(q_pre)) * (HG_DK ** -0.5)
    f = lb + (1.0 - lb) * _sigmoid(proj(C_HF, HG_WIDTH))
    hk_ref[...] = 1.0 - f
    hlf_ref[...] = jnp.log2(f)
    g_pre = proj(C_HG, HG_WIDTH)
    hg_ref[...] = ((g_pre * _sigmoid(g_pre)) * og_ref[...]).astype(BF16)

    def gate(dst, c_w, part):
        w = D_MODEL // 2
        dst[:, part * w:(part + 1) * w] = _sigmoid(proj(c_w + part * w, w)).astype(BF16)

    assert C_AKV == C_AQ + ATT_WIDTH
    aqkv = proj(C_AQ, ATT_WIDTH + 2 * ATT_KV_WIDTH)
    a_q, kv = aqkv[:, :ATT_WIDTH], aqkv[:, ATT_WIDTH:]
    gate(ga_ref, C_GA, 0)
    sq = (a_q * a_q).astype(BF16)
    nbd = bdq_ref.shape[0]
    ms_q = jnp.concatenate([_dot(sq[:, c:c + nbd], bdq_ref[...]) for c in range(0, ATT_WIDTH, nbd)],
                           axis=1) * (1.0 / ATT_HD)
    aq_ref[...] = (((a_q * lax.rsqrt(ms_q + EPS)) * qg_ref[...]) * (LOG2E * ATT_HD ** -0.5)).astype(BF16)
    gate(ga_ref, C_GA, 1)
    a_k = kv[:, :ATT_KV_WIDTH]
    ms_k = _dot((a_k * a_k).astype(BF16), bdk_ref[...]) * (1.0 / ATT_HD)
    lo = lax.broadcasted_iota(jnp.int32, (1, LANES), 1) < ATT_HD
    for src, dst in (((a_k * lax.rsqrt(ms_k + EPS)) * kg_ref[...], ak_ref), (kv[:, ATT_KV_WIDTH:], av_ref)):
        swapped = pltpu.roll(src, ATT_HD, axis=1)
        dst[:, :LANES] = jnp.where(lo, src, swapped).astype(BF16)
        dst[:, LANES:] = jnp.where(lo, swapped, src).astype(BF16)
    gate(gb_ref, C_GB, 0)
    gate(gb_ref, C_GB, 1)
    hv_ref[...] = proj(C_HI, HG_WIDTH).astype(BF16)


def _inproj(x2, g1, w, lbl, og, qg, kg, bdq, bdk, tm):
    m = x2.shape[0]
    row = lambda c: pl.BlockSpec((tm, c), lambda i: (i, 0))
    full = lambda a: pl.BlockSpec(a.shape, lambda i: (0,) * a.ndim)
    outs = [
        (HG_WIDTH, F32), (HG_WIDTH, F32), (HG_WIDTH, F32), (HG_WIDTH, BF16), (HG_WIDTH, BF16),
        (ATT_WIDTH, BF16), (2 * LANES, BF16), (2 * LANES, BF16), (D_MODEL, BF16), (D_MODEL, BF16),
    ]
    return pl.pallas_call(
        _inproj_kernel,
        grid=(m // tm,),
        in_specs=[row(D_MODEL), full(g1),
                  pl.BlockSpec(w.shape, lambda i: (0, 0), pipeline_mode=pl.Buffered(1)),
                  full(lbl), full(og), full(qg), full(kg), full(bdq), full(bdk)],
        out_specs=[row(c) for c, _ in outs],
        out_shape=[jax.ShapeDtypeStruct((m, c), dt) for c, dt in outs],
        scratch_shapes=[pltpu.VMEM(w.shape, BF16)],
        compiler_params=pltpu.CompilerParams(dimension_semantics=("arbitrary",), vmem_limit_bytes=VMEM_LIMIT),
        name="inproj",
    )(x2, g1, w, lbl, og, qg, kg, bdq, bdk)


def _hgrn_head(q, k, v, g, sg, st, level):
    c = HG_CHUNK
    n = HG_TILE // c
    qs = [q[i * c:(i + 1) * c].astype(BF16) for i in range(n)]
    ks = [k[i * c:(i + 1) * c].astype(BF16) for i in range(n)]
    gs = [g[i * c:(i + 1) * c] for i in range(n)]
    mid = [gc[c // 2 - 1:c // 2] for gc in gs]
    end = [gc[c - 1:c] for gc in gs]
    p2 = lambda x: jnp.exp2(x).astype(BF16)
    cat = lambda parts: jnp.concatenate(parts, axis=0)
    zero = jnp.zeros((c, HG_DK), BF16)

    s1 = _dot_nt(cat([qs[i] * p2(gs[i] - mid[i]) for i in range(n)]),
                 cat([ks[i] * p2(mid[i] - gs[i]) for i in range(n)]))
    k2_0 = ks[0] * p2(end[0] - gs[0])
    k2_2 = ks[2] * p2(end[2] - gs[2])
    q2_1 = qs[1] * p2(gs[1] - end[0])
    q2_3 = qs[3] * p2(gs[3] - end[2])
    s2 = _dot_nt(cat([zero, q2_1, zero, q2_3]), cat([k2_0, zero, k2_2, zero]))
    k3_0 = k2_0 * p2(end[1] - end[0])
    k3_1 = ks[1] * p2(end[1] - gs[1])
    q3_2 = qs[2] * p2(gs[2] - end[1])
    q3_3 = q2_3 * p2(end[2] - end[1])
    s3 = _dot_nt(cat([zero, zero, q3_2, q3_3]), cat([k3_0, k3_1, zero, zero]))
    scores = jnp.where(level == 1, s1, jnp.where(level == 2, s2, s3))
    o = _dot(scores.astype(BF16), v)

    d_half = p2(end[3] - end[1])
    k_in = cat([k3_0 * d_half, k3_1 * d_half, k2_2 * p2(end[3] - end[2]), ks[3] * p2(end[3] - gs[3])])
    q_in = cat([qs[0] * p2(gs[0]), q2_1 * p2(end[0]), q3_2 * p2(end[1]), q3_3 * p2(end[1])])
    o = o + _dot_nt(q_in, st.astype(BF16))
    st_new = st * jnp.exp2(end[3]) + _dot_tn(v, k_in)

    ms = jnp.mean(o * o, axis=-1, keepdims=True)
    return (o * lax.rsqrt(ms + EPS)) * sg, st_new


def _hgrn_prefix(i, lf_ref, tri_ref):
    lf = lf_ref[i * HG_TILE:(i + 1) * HG_TILE, :]
    lf_hi = lf.astype(BF16)
    lf_lo = (lf - lf_hi.astype(F32)).astype(BF16)
    return _dot(tri_ref[...], jnp.concatenate([lf_hi, lf_lo], axis=0))


def _hgrn_step(i, h, g_all, q_ref, k_ref, v_ref, sg_ref, level, o_ref, st_ref):
    rs = slice(i * HG_TILE, (i + 1) * HG_TILE)
    cs = slice(h * HG_DK, (h + 1) * HG_DK)
    o, st_new = _hgrn_head(q_ref[rs, cs], k_ref[rs, cs], v_ref[rs, cs], g_all[:, cs],
                           sg_ref[rs, cs].astype(F32), st_ref[h], level)
    st_ref[h] = st_new
    o_ref[rs, cs] = o.astype(BF16)


def _swa_step(i, j, first_tile, sink_ref, q_ref, kc_ref, kp_ref, vc_ref, vp_ref, bias_ref, o_ref):
    w = WINDOW
    lane = lax.broadcasted_iota(jnp.int32, (1, LANES), 1)
    lo_b = (lane < ATT_HD).astype(BF16)
    hi_b = (lane >= ATT_HD).astype(BF16)
    lo = lax.broadcasted_iota(jnp.int32, (w, LANES), 1) < ATT_HD
    row0 = lax.broadcasted_iota(jnp.int32, (SUBLANES, LANES), 0) == 0
    not_row0 = (lax.broadcasted_iota(jnp.int32, (BF16_ROWS, LANES), 0) > 0).astype(BF16)

    kvh = j // (ATT_GROUP // 2)
    cs = slice(kvh * LANES, (kvh + 1) * LANES)
    if i == 0:
        kk = jnp.concatenate([kp_ref[:, cs], kc_ref[0:w, cs]], axis=0)
        vv = jnp.concatenate([vp_ref[0:BF16_ROWS, cs] * not_row0, vp_ref[BF16_ROWS:, cs], vc_ref[0:w, cs]], axis=0)
    else:
        r0 = (i - 1) * w
        kk = kc_ref[r0:r0 + 2 * w, cs]
        vv = jnp.concatenate([vc_ref[r0:r0 + BF16_ROWS, cs] * not_row0,
                              vc_ref[r0 + BF16_ROWS:r0 + 2 * w, cs]], axis=0)
    qc = q_ref[i * w:(i + 1) * w, j * LANES:(j + 1) * LANES]
    qs = jnp.concatenate([qc * lo_b, qc * hi_b], axis=0)
    st = _dot_nt(kk, qs) + bias_ref[j]
    if i == 0:
        st = jnp.concatenate([st[:w] + jnp.where(first_tile, NEG, 0.0), st[w:]], axis=0)
    sink = jnp.concatenate([jnp.broadcast_to(sink_ref[2 * j + n] * LOG2E, (SUBLANES, LANES)) for n in range(2)],
                           axis=1)
    st = jnp.concatenate([jnp.where(jnp.concatenate([row0, row0], axis=1), sink, st[:SUBLANES]), st[SUBLANES:]],
                         axis=0)
    m = jnp.max(st, axis=0, keepdims=True)
    e = jnp.exp2(st - m)
    r = 1.0 / jnp.sum(e, axis=0, keepdims=True)
    o2 = _dot_tn(e.astype(BF16) * r.astype(BF16), vv)
    o_ref[i * w:(i + 1) * w, j * LANES:(j + 1) * LANES] = jnp.where(lo, o2[:w], o2[w:]).astype(BF16)


def _seq_kernel(sink_ref, hq_ref, hk_ref, hlf_ref, hv_ref, hg_ref, tri_ref, lvl_ref,
                aq_ref, kc_ref, kp_ref, vc_ref, vp_ref, bias_ref, oa_ref, ob_ref, st_ref, *, nsub):
    first_tile = pl.program_id(1) == 0

    @pl.when(first_tile)
    def _():
        st_ref[...] = jnp.zeros_like(st_ref)

    level = lvl_ref[...]
    assert HG_HEADS == ATT_HEADS // 2 and HG_TILE == WINDOW
    for i in range(nsub):
        g_all = _hgrn_prefix(i, hlf_ref, tri_ref)
        for h in range(HG_HEADS):
            _hgrn_step(i, h, g_all, hq_ref, hk_ref, hv_ref, hg_ref, level, oa_ref, st_ref)
            _swa_step(i, h, first_tile, sink_ref, aq_ref, kc_ref, kp_ref, vc_ref, vp_ref, bias_ref, ob_ref)


def _hgrn_consts():
    t = HG_TILE
    row = np.arange(t)[:, None]
    col = np.arange(t)[None, :]
    tri = (col <= row).astype(np.float32)
    ci, cj = row // HG_CHUNK, col // HG_CHUNK
    level = np.where((ci == cj) & (col <= row), 1, np.where((ci == cj + 1) & (cj % 2 == 0), 2, 0))
    return jnp.asarray(np.concatenate([tri, tri], axis=1), dtype=BF16), jnp.asarray(level, dtype=jnp.int32)


def _swa_bias():
    w = WINDOW
    i = np.arange(w)[:, None]
    j = np.arange(2 * w)[None, :]
    delta = (i + w - j).astype(np.float32)
    band = (delta >= 0) & (delta < w)
    assert not band[:, 0].any()
    slopes = 2.0 ** (-8.0 * np.arange(1, ATT_HEADS + 1, dtype=np.float32) / ATT_HEADS)
    per_head = np.where(band[None], -slopes[:, None, None] * delta[None] * np.float32(LOG2E), np.float32(NEG))
    pairs = per_head.astype(np.float32).reshape(ATT_HEADS // 2, 2 * w, 2 * w)
    return jnp.asarray(np.ascontiguousarray(pairs.transpose(0, 2, 1)))


def _seq(sinks, hq, hk, hlf, hv, hg, aq, ak, av, ts):
    b, s, _ = hq.shape
    tri, level = _hgrn_consts()
    bias = _swa_bias()
    nprev = ts // WINDOW
    wide = pl.BlockSpec((None, ts, HG_WIDTH), lambda bi, si, *_: (bi, si, 0))
    cur = pl.BlockSpec((None, ts, 2 * LANES), lambda bi, si, *_: (bi, si, 0))
    prev = pl.BlockSpec((None, WINDOW, 2 * LANES), lambda bi, si, *_: (bi, jnp.maximum(si * nprev - 1, 0), 0))
    full = lambda a: pl.BlockSpec(a.shape, lambda bi, si, *_: (0,) * a.ndim)
    return pl.pallas_call(
        functools.partial(_seq_kernel, nsub=ts // HG_TILE),
        grid_spec=pltpu.PrefetchScalarGridSpec(
            num_scalar_prefetch=1,
            grid=(b, s // ts),
            in_specs=[wide, wide, wide, wide, wide, full(tri), full(level),
                      wide, cur, prev, cur, prev, full(bias)],
            out_specs=[wide, wide],
            scratch_shapes=[pltpu.VMEM((HG_HEADS, HG_DV, HG_DK), F32)],
        ),
        out_shape=[jax.ShapeDtypeStruct((b, s, HG_WIDTH), BF16), jax.ShapeDtypeStruct((b, s, ATT_WIDTH), BF16)],
        compiler_params=pltpu.CompilerParams(dimension_semantics=("arbitrary", "arbitrary"),
                                             vmem_limit_bytes=VMEM_LIMIT),
        name="seq",
    )(sinks, hq, hk, hlf, hv, hg, tri, level, aq, ak, ak, av, av, bias)


def _post_kernel(x_ref, oa_ref, ob_ref, ga_ref, gb_ref, wa_ref, wb_ref, wo_ref, g2_ref,
                 wu_ref, cw_ref, cb_ref, wd_ref, o_ref, x1_ref, h2_ref, gs_ref, carry_ref, act_ref, *, tm):
    pad = SUBLANES

    @pl.when(pl.program_id(1) == 0)
    def _():
        carry_ref[...] = jnp.zeros_like(carry_ref)

    mixed = (ga_ref[...].astype(F32) * _dot(oa_ref[...], wa_ref[...])
             + gb_ref[...].astype(F32) * _dot(ob_ref[...], wb_ref[...]))
    x1_ref[...] = x_ref[...] + _dot(mixed.astype(BF16), wo_ref[...])
    h2_ref[...] = _rms(x1_ref[...], g2_ref[...]).astype(BF16)

    h2 = h2_ref[...]
    for c0 in range(0, D_FF, FF_CHUNK):
        cw = min(FF_CHUNK, D_FF - c0)
        cs = slice(c0, c0 + cw)
        gate = _dot(h2, wu_ref[:, cs])
        val = _dot(h2, wu_ref[:, D_FF + c0:D_FF + c0 + cw])
        gs_ref[0:pad, :cw] = carry_ref[:, cs]
        gs_ref[pad:pad + tm, :cw] = gate
        carry_ref[:, cs] = gate[tm - pad:tm, :]
        conv = cb_ref[:, cs] + cw_ref[2:3, cs] * gate
        conv = conv + cw_ref[1:2, cs] * gs_ref[pad - 1:pad - 1 + tm, :cw]
        conv = conv + cw_ref[0:1, cs] * gs_ref[pad - 2:pad - 2 + tm, :cw]
        gelu = (0.5 * conv) * (1.0 + lax.erf(conv * (0.5 ** 0.5)))
        act_ref[:, cs] = (gelu * val).astype(BF16)
    o_ref[...] = x1_ref[...] + _dot(act_ref[...], wd_ref[...])


def _post(x, oa, ob, ga, gb, wa, wb, wo, g2, wu, cw, cb, wd, tm):
    b, s, _ = x.shape
    row = lambda c: pl.BlockSpec((None, tm, c), lambda bi, si: (bi, si, 0))
    full = lambda a: pl.BlockSpec(a.shape, lambda bi, si: (0,) * a.ndim, pipeline_mode=pl.Buffered(1))
    return pl.pallas_call(
        functools.partial(_post_kernel, tm=tm),
        grid=(b, s // tm),
        in_specs=[row(D_MODEL), row(HG_WIDTH), row(ATT_WIDTH), row(D_MODEL), row(D_MODEL),
                  full(wa), full(wb), full(wo), full(g2), full(wu), full(cw), full(cb), full(wd)],
        out_specs=row(D_MODEL),
        out_shape=jax.ShapeDtypeStruct((b, s, D_MODEL), F32),
        scratch_shapes=[pltpu.VMEM((tm, D_MODEL), F32),
                        pltpu.VMEM((tm, D_MODEL), BF16),
                        pltpu.VMEM((tm + SUBLANES, FF_CHUNK), F32),
                        pltpu.VMEM((SUBLANES, D_FF), F32),
                        pltpu.VMEM((tm, D_FF), BF16)],
        compiler_params=pltpu.CompilerParams(dimension_semantics=("arbitrary", "arbitrary"),
                                             vmem_limit_bytes=VMEM_LIMIT),
        name="post",
    )(x, oa, ob, ga, gb, wa, wb, wo, g2, wu, cw, cb, wd)


def _block_diag_ones(n, blk):
    idx = np.arange(n) // blk
    return jnp.asarray((idx[:, None] == idx[None, :]).astype(np.float32), dtype=BF16)


def kernel(x, norm1_g, w_in, hgrn_lb_logits, hgrn_out_g, q_norm_g, k_norm_g, attn_sinks,
           w_branch_a, w_branch_b, w_out, norm2_g, w_up, conv_w, conv_b, w_down):
    b, s, d = x.shape
    assert d == D_MODEL and w_in.shape == (1, D_MODEL, IN_COLS)
    m = b * s
    tm, ts = _tiles(s)

    og = jnp.tile(hgrn_out_g[0], HG_HEADS).reshape(1, HG_WIDTH)
    qg = jnp.tile(q_norm_g[0], ATT_HEADS).reshape(1, ATT_WIDTH)
    kg = jnp.tile(k_norm_g[0], ATT_KV_HEADS).reshape(1, ATT_KV_WIDTH)
    hq, hk, hlf, hv, hg, aq, ak, av, ga, gb = _inproj(
        x.reshape(m, d), norm1_g[0].reshape(1, d), w_in[0], hgrn_lb_logits, og, qg, kg,
        _block_diag_ones(2 * LANES, ATT_HD), _block_diag_ones(ATT_KV_WIDTH, ATT_HD), tm)

    r3 = lambda a: a.reshape(b, s, a.shape[-1])
    o_a, o_b = _seq(attn_sinks[0], r3(hq), r3(hk), r3(hlf), r3(hv), r3(hg), r3(aq), r3(ak), r3(av), ts)

    return _post(x, o_a, o_b, r3(ga), r3(gb),
                 w_branch_a[0].astype(BF16), w_branch_b[0].astype(BF16), w_out[0].astype(BF16),
                 norm2_g[0].reshape(1, d), w_up[0].astype(BF16),
                 conv_w[0], conv_b[0].reshape(1, D_FF), w_down[0].astype(BF16), tm)
```

```python
import functools

import numpy as np
import jax
import jax.numpy as jnp
from jax import lax
from jax.experimental import pallas as pl
from jax.experimental.pallas import tpu as pltpu

F32 = jnp.float32
BF16 = jnp.bfloat16

D_MODEL = 1024
HG_HEADS = 4
HG_DK = 128
HG_DV = 128
HG_WIDTH = HG_HEADS * HG_DK
HG_CHUNK = 32
ATT_HEADS = 8
ATT_KV_HEADS = 2
ATT_GROUP = ATT_HEADS // ATT_KV_HEADS
ATT_HD = 64
ATT_WIDTH = ATT_HEADS * ATT_HD
ATT_KV_WIDTH = ATT_KV_HEADS * ATT_HD
WINDOW = 128
D_FF = 2816
CONV_W = 3
EPS = 1e-6
NEG = -1e30
LOG2E = 1.4426950408889634

LANES = 128
SUBLANES = 8
BF16_ROWS = 2 * SUBLANES
V7X_VMEM_BYTES = 64 * 1024 * 1024
VMEM_LIMIT = V7X_VMEM_BYTES * 7 // 8
HG_TILE = 128
FF_CHUNK = 256
ROW_TILE = 512


def _tiles(s):
    assert s % ROW_TILE == 0 and ROW_TILE % HG_TILE == 0
    return ROW_TILE, 2 * ROW_TILE if s % (2 * ROW_TILE) == 0 else ROW_TILE

C_HQ, C_HF, C_HI, C_HG = 0, HG_WIDTH, 2 * HG_WIDTH, 3 * HG_WIDTH
C_AQ = 4 * HG_WIDTH
C_AKV = C_AQ + ATT_WIDTH
C_GA = C_AKV + 2 * ATT_KV_WIDTH
C_GB = C_GA + D_MODEL
IN_COLS = C_GB + D_MODEL


def _dot(a, b):
    return jnp.dot(a, b, preferred_element_type=F32)


def _dot_nt(a, b):
    return lax.dot_general(a, b, (((1,), (1,)), ((), ())), preferred_element_type=F32)


def _dot_tn(a, b):
    return lax.dot_general(a, b, (((0,), (0,)), ((), ())), preferred_element_type=F32)


def _sigmoid(x):
    return 0.5 * jnp.tanh(0.5 * x) + 0.5


def _rms(x, g):
    ms = jnp.mean(x * x, axis=-1, keepdims=True)
    return (x * lax.rsqrt(ms + EPS)) * g


def _inproj_kernel(x_ref, g1_ref, w_ref, lbl_ref, og_ref, qg_ref, kg_ref, bdq_ref, bdk_ref,
                   hq_ref, hk_ref, hlf_ref, hv_ref, hg_ref, aq_ref, ak_ref, av_ref, ga_ref, gb_ref, wb_ref):
    @pl.when(pl.program_id(0) == 0)
    def _():
        wb_ref[...] = w_ref[...].astype(BF16)

    h = _rms(x_ref[...], g1_ref[...]).astype(BF16)

    def proj(c0, width):
        return _dot(h, wb_ref[:, c0:c0 + width])

    lbl = lbl_ref[...]
    e = jnp.exp(lbl - jnp.max(lbl, axis=0, keepdims=True))
    lb = e[0:1] / jnp.sum(e, axis=0, keepdims=True)

    q_pre = proj(C_HQ, HG_WIDTH)
    hq_ref[...] = (q_pre * _sigmoid(q_pre)) * (HG_DK ** -0.5)
    f = lb + (1.0 - lb) * _sigmoid(proj(C_HF, HG_WIDTH))
    hk_ref[...] = 1.0 - f
    hlf_ref[...] = jnp.log2(f)
    g_pre = proj(C_HG, HG_WIDTH)
    hg_ref[...] = ((g_pre * _sigmoid(g_pre)) * og_ref[...]).astype(BF16)

    def gate(dst, c_w, part):
        w = D_MODEL // 2
        dst[:, part * w:(part + 1) * w] = _sigmoid(proj(c_w + part * w, w)).astype(BF16)

    assert C_AKV == C_AQ + ATT_WIDTH
    aqkv = proj(C_AQ, ATT_WIDTH + 2 * ATT_KV_WIDTH)
    a_q, kv = aqkv[:, :ATT_WIDTH], aqkv[:, ATT_WIDTH:]
    gate(ga_ref, C_GA, 0)
    sq = (a_q * a_q).astype(BF16)
    nbd = bdq_ref.shape[0]
    ms_q = jnp.concatenate([_dot(sq[:, c:c + nbd], bdq_ref[...]) for c in range(0, ATT_WIDTH, nbd)],
                           axis=1) * (1.0 / ATT_HD)
    aq_ref[...] = (((a_q * lax.rsqrt(ms_q + EPS)) * qg_ref[...]) * (LOG2E * ATT_HD ** -0.5)).astype(BF16)
    gate(ga_ref, C_GA, 1)
    a_k = kv[:, :ATT_KV_WIDTH]
    ms_k = _dot((a_k * a_k).astype(BF16), bdk_ref[...]) * (1.0 / ATT_HD)
    lo = lax.broadcasted_iota(jnp.int32, (1, LANES), 1) < ATT_HD
    for src, dst in (((a_k * lax.rsqrt(ms_k + EPS)) * kg_ref[...], ak_ref), (kv[:, ATT_KV_WIDTH:], av_ref)):
        swapped = pltpu.roll(src, ATT_HD, axis=1)
        dst[:, :LANES] = jnp.where(lo, src, swapped).astype(BF16)
        dst[:, LANES:] = jnp.where(lo, swapped, src).astype(BF16)
    gate(gb_ref, C_GB, 0)
    gate(gb_ref, C_GB, 1)
    hv_ref[...] = proj(C_HI, HG_WIDTH).astype(BF16)


def _inproj(x2, g1, w, lbl, og, qg, kg, bdq, bdk, tm):
    m = x2.shape[0]
    row = lambda c: pl.BlockSpec((tm, c), lambda i: (i, 0))
    full = lambda a: pl.BlockSpec(a.shape, lambda i: (0,) * a.ndim)
    outs = [
        (HG_WIDTH, F32), (HG_WIDTH, F32), (HG_WIDTH, F32), (HG_WIDTH, BF16), (HG_WIDTH, BF16),
        (ATT_WIDTH, BF16), (2 * LANES, BF16), (2 * LANES, BF16), (D_MODEL, BF16), (D_MODEL, BF16),
    ]
    return pl.pallas_call(
        _inproj_kernel,
        grid=(m // tm,),
        in_specs=[row(D_MODEL), full(g1),
                  pl.BlockSpec(w.shape, lambda i: (0, 0), pipeline_mode=pl.Buffered(1)),
                  full(lbl), full(og), full(qg), full(kg), full(bdq), full(bdk)],
        out_specs=[row(c) for c, _ in outs],
        out_shape=[jax.ShapeDtypeStruct((m, c), dt) for c, dt in outs],
        scratch_shapes=[pltpu.VMEM(w.shape, BF16)],
        compiler_params=pltpu.CompilerParams(dimension_semantics=("arbitrary",), vmem_limit_bytes=VMEM_LIMIT),
        name="inproj",
    )(x2, g1, w, lbl, og, qg, kg, bdq, bdk)


def _hgrn_head(q, k, v, g, sg, st, level):
    c = HG_CHUNK
    n = HG_TILE // c
    qs = [q[i * c:(i + 1) * c].astype(BF16) for i in range(n)]
    ks = [k[i * c:(i + 1) * c].astype(BF16) for i in range(n)]
    gs = [g[i * c:(i + 1) * c] for i in range(n)]
    mid = [gc[c // 2 - 1:c // 2] for gc in gs]
    end = [gc[c - 1:c] for gc in gs]
    p2 = lambda x: jnp.exp2(x).astype(BF16)
    cat = lambda parts: jnp.concatenate(parts, axis=0)
    zero = jnp.zeros((c, HG_DK), BF16)

    q1 = [qs[i] * p2(gs[i] - mid[i]) for i in range(n)]
    k1 = [ks[i] * p2(mid[i] - gs[i]) for i in range(n)]
    s1 = _dot_nt(cat(q1), cat(k1))
    q_to = lambda i, x: q1[i] * p2(mid[i] - x)
    k_to = lambda i, x: k1[i] * p2(x - mid[i])
    s2 = _dot_nt(cat([zero, q_to(1, end[0]), zero, q_to(3, end[2])]),
                 cat([k_to(0, end[0]), zero, k_to(2, end[2]), zero]))
    s3 = _dot_nt(cat([zero, zero, q_to(2, end[1]), q_to(3, end[1])]),
                 cat([k_to(0, end[1]), k_to(1, end[1]), zero, zero]))
    scores = jnp.where(level == 1, s1, jnp.where(level == 2, s2, s3))
    o = _dot(scores.astype(BF16), v)

    k_in = cat([k_to(i, end[n - 1]) for i in range(n)])
    q_in = cat([q1[i] * p2(mid[i]) for i in range(n)])
    o = o + _dot_nt(q_in, st.astype(BF16))
    st_new = st * jnp.exp2(end[3]) + _dot_tn(v, k_in)

    ms = jnp.mean(o * o, axis=-1, keepdims=True)
    return (o * lax.rsqrt(ms + EPS)) * sg, st_new


def _hgrn_prefix(i, lf_ref, tri_ref):
    lf = lf_ref[i * HG_TILE:(i + 1) * HG_TILE, :]
    lf_hi = lf.astype(BF16)
    lf_lo = (lf - lf_hi.astype(F32)).astype(BF16)
    return _dot(tri_ref[...], jnp.concatenate([lf_hi, lf_lo], axis=0))


def _hgrn_step(i, h, g_all, q_ref, k_ref, v_ref, sg_ref, level, o_ref, st_ref):
    rs = slice(i * HG_TILE, (i + 1) * HG_TILE)
    cs = slice(h * HG_DK, (h + 1) * HG_DK)
    o, st_new = _hgrn_head(q_ref[rs, cs], k_ref[rs, cs], v_ref[rs, cs], g_all[:, cs],
                           sg_ref[rs, cs].astype(F32), st_ref[h], level)
    st_ref[h] = st_new
    o_ref[rs, cs] = o.astype(BF16)


def _swa_step(i, j, first_tile, sink_ref, q_ref, kc_ref, kp_ref, vc_ref, vp_ref, bias_ref, o_ref):
    w = WINDOW
    lane = lax.broadcasted_iota(jnp.int32, (1, LANES), 1)
    lo_b = (lane < ATT_HD).astype(BF16)
    hi_b = (lane >= ATT_HD).astype(BF16)
    lo = lax.broadcasted_iota(jnp.int32, (w, LANES), 1) < ATT_HD
    row0 = lax.broadcasted_iota(jnp.int32, (SUBLANES, LANES), 0) == 0
    not_row0 = (lax.broadcasted_iota(jnp.int32, (BF16_ROWS, LANES), 0) > 0).astype(BF16)

    kvh = j // (ATT_GROUP // 2)
    cs = slice(kvh * LANES, (kvh + 1) * LANES)
    if i == 0:
        kk = jnp.concatenate([kp_ref[:, cs], kc_ref[0:w, cs]], axis=0)
        vv = jnp.concatenate([vp_ref[0:BF16_ROWS, cs] * not_row0, vp_ref[BF16_ROWS:, cs], vc_ref[0:w, cs]], axis=0)
    else:
        r0 = (i - 1) * w
        kk = kc_ref[r0:r0 + 2 * w, cs]
        vv = jnp.concatenate([vc_ref[r0:r0 + BF16_ROWS, cs] * not_row0,
                              vc_ref[r0 + BF16_ROWS:r0 + 2 * w, cs]], axis=0)
    qc = q_ref[i * w:(i + 1) * w, j * LANES:(j + 1) * LANES]
    qs = jnp.concatenate([qc * lo_b, qc * hi_b], axis=0)
    st = _dot_nt(kk, qs) + bias_ref[j]
    if i == 0:
        st = jnp.concatenate([st[:w] + jnp.where(first_tile, NEG, 0.0), st[w:]], axis=0)
    sink = jnp.concatenate([jnp.broadcast_to(sink_ref[2 * j + n] * LOG2E, (SUBLANES, LANES)) for n in range(2)],
                           axis=1)
    st = jnp.concatenate([jnp.where(jnp.concatenate([row0, row0], axis=1), sink, st[:SUBLANES]), st[SUBLANES:]],
                         axis=0)
    m = jnp.max(st, axis=0, keepdims=True)
    e = jnp.exp2(st - m)
    r = 1.0 / jnp.sum(e, axis=0, keepdims=True)
    o2 = _dot_tn(e.astype(BF16) * r.astype(BF16), vv)
    o_ref[i * w:(i + 1) * w, j * LANES:(j + 1) * LANES] = jnp.where(lo, o2[:w], o2[w:]).astype(BF16)


def _seq_kernel(sink_ref, hq_ref, hk_ref, hlf_ref, hv_ref, hg_ref, tri_ref, lvl_ref,
                aq_ref, kc_ref, kp_ref, vc_ref, vp_ref, bias_ref, oa_ref, ob_ref, st_ref, *, nsub):
    first_tile = pl.program_id(1) == 0

    @pl.when(first_tile)
    def _():
        st_ref[...] = jnp.zeros_like(st_ref)

    level = lvl_ref[...]
    assert HG_HEADS == ATT_HEADS // 2 and HG_TILE == WINDOW
    for i in range(nsub):
        g_all = _hgrn_prefix(i, hlf_ref, tri_ref)
        for h in range(HG_HEADS):
            _hgrn_step(i, h, g_all, hq_ref, hk_ref, hv_ref, hg_ref, level, oa_ref, st_ref)
            _swa_step(i, h, first_tile, sink_ref, aq_ref, kc_ref, kp_ref, vc_ref, vp_ref, bias_ref, ob_ref)


def _hgrn_consts():
    t = HG_TILE
    row = np.arange(t)[:, None]
    col = np.arange(t)[None, :]
    tri = (col <= row).astype(np.float32)
    ci, cj = row // HG_CHUNK, col // HG_CHUNK
    level = np.where((ci == cj) & (col <= row), 1, np.where((ci == cj + 1) & (cj % 2 == 0), 2, 0))
    return jnp.asarray(np.concatenate([tri, tri], axis=1), dtype=BF16), jnp.asarray(level, dtype=jnp.int32)


def _swa_bias():
    w = WINDOW
    i = np.arange(w)[:, None]
    j = np.arange(2 * w)[None, :]
    delta = (i + w - j).astype(np.float32)
    band = (delta >= 0) & (delta < w)
    assert not band[:, 0].any()
    slopes = 2.0 ** (-8.0 * np.arange(1, ATT_HEADS + 1, dtype=np.float32) / ATT_HEADS)
    per_head = np.where(band[None], -slopes[:, None, None] * delta[None] * np.float32(LOG2E), np.float32(NEG))
    pairs = per_head.astype(np.float32).reshape(ATT_HEADS // 2, 2 * w, 2 * w)
    return jnp.asarray(np.ascontiguousarray(pairs.transpose(0, 2, 1)))


def _seq(sinks, hq, hk, hlf, hv, hg, aq, ak, av, ts):
    b, s, _ = hq.shape
    tri, level = _hgrn_consts()
    bias = _swa_bias()
    nprev = ts // WINDOW
    wide = pl.BlockSpec((None, ts, HG_WIDTH), lambda bi, si, *_: (bi, si, 0))
    cur = pl.BlockSpec((None, ts, 2 * LANES), lambda bi, si, *_: (bi, si, 0))
    prev = pl.BlockSpec((None, WINDOW, 2 * LANES), lambda bi, si, *_: (bi, jnp.maximum(si * nprev - 1, 0), 0))
    full = lambda a: pl.BlockSpec(a.shape, lambda bi, si, *_: (0,) * a.ndim)
    return pl.pallas_call(
        functools.partial(_seq_kernel, nsub=ts // HG_TILE),
        grid_spec=pltpu.PrefetchScalarGridSpec(
            num_scalar_prefetch=1,
            grid=(b, s // ts),
            in_specs=[wide, wide, wide, wide, wide, full(tri), full(level),
                      wide, cur, prev, cur, prev, full(bias)],
            out_specs=[wide, wide],
            scratch_shapes=[pltpu.VMEM((HG_HEADS, HG_DV, HG_DK), F32)],
        ),
        out_shape=[jax.ShapeDtypeStruct((b, s, HG_WIDTH), BF16), jax.ShapeDtypeStruct((b, s, ATT_WIDTH), BF16)],
        compiler_params=pltpu.CompilerParams(dimension_semantics=("arbitrary", "arbitrary"),
                                             vmem_limit_bytes=VMEM_LIMIT),
        name="seq",
    )(sinks, hq, hk, hlf, hv, hg, tri, level, aq, ak, ak, av, av, bias)


def _post_kernel(x_ref, oa_ref, ob_ref, ga_ref, gb_ref, wa_ref, wb_ref, wo_ref, g2_ref,
                 wu_ref, cw_ref, cb_ref, wd_ref, o_ref, x1_ref, h2_ref, gs_ref, carry_ref, act_ref, *, tm):
    pad = SUBLANES

    @pl.when(pl.program_id(1) == 0)
    def _():
        carry_ref[...] = jnp.zeros_like(carry_ref)

    mixed = (ga_ref[...].astype(F32) * _dot(oa_ref[...], wa_ref[...])
             + gb_ref[...].astype(F32) * _dot(ob_ref[...], wb_ref[...]))
    x1_ref[...] = x_ref[...] + _dot(mixed.astype(BF16), wo_ref[...])
    h2_ref[...] = _rms(x1_ref[...], g2_ref[...]).astype(BF16)

    h2 = h2_ref[...]
    for c0 in range(0, D_FF, FF_CHUNK):
        cw = min(FF_CHUNK, D_FF - c0)
        cs = slice(c0, c0 + cw)
        gate = _dot(h2, wu_ref[:, cs])
        val = _dot(h2, wu_ref[:, D_FF + c0:D_FF + c0 + cw])
        gs_ref[0:pad, :cw] = carry_ref[:, cs]
        gs_ref[pad:pad + tm, :cw] = gate
        carry_ref[:, cs] = gate[tm - pad:tm, :]
        conv = cb_ref[:, cs] + cw_ref[2:3, cs] * gate
        conv = conv + cw_ref[1:2, cs] * gs_ref[pad - 1:pad - 1 + tm, :cw]
        conv = conv + cw_ref[0:1, cs] * gs_ref[pad - 2:pad - 2 + tm, :cw]
        gelu = (0.5 * conv) * (1.0 + lax.erf(conv * (0.5 ** 0.5)))
        act_ref[:, cs] = (gelu * val).astype(BF16)
    o_ref[...] = x1_ref[...] + _dot(act_ref[...], wd_ref[...])


def _post(x, oa, ob, ga, gb, wa, wb, wo, g2, wu, cw, cb, wd, tm):
    b, s, _ = x.shape
    row = lambda c: pl.BlockSpec((None, tm, c), lambda bi, si: (bi, si, 0))
    full = lambda a: pl.BlockSpec(a.shape, lambda bi, si: (0,) * a.ndim, pipeline_mode=pl.Buffered(1))
    return pl.pallas_call(
        functools.partial(_post_kernel, tm=tm),
        grid=(b, s // tm),
        in_specs=[row(D_MODEL), row(HG_WIDTH), row(ATT_WIDTH), row(D_MODEL), row(D_MODEL),
                  full(wa), full(wb), full(wo), full(g2), full(wu), full(cw), full(cb), full(wd)],
        out_specs=row(D_MODEL),
        out_shape=jax.ShapeDtypeStruct((b, s, D_MODEL), F32),
        scratch_shapes=[pltpu.VMEM((tm, D_MODEL), F32),
                        pltpu.VMEM((tm, D_MODEL), BF16),
                        pltpu.VMEM((tm + SUBLANES, FF_CHUNK), F32),
                        pltpu.VMEM((SUBLANES, D_FF), F32),
                        pltpu.VMEM((tm, D_FF), BF16)],
        compiler_params=pltpu.CompilerParams(dimension_semantics=("arbitrary", "arbitrary"),
                                             vmem_limit_bytes=VMEM_LIMIT),
        name="post",
    )(x, oa, ob, ga, gb, wa, wb, wo, g2, wu, cw, cb, wd)


def _block_diag_ones(n, blk):
    idx = np.arange(n) // blk
    return jnp.asarray((idx[:, None] == idx[None, :]).astype(np.float32), dtype=BF16)


def kernel(x, norm1_g, w_in, hgrn_lb_logits, hgrn_out_g, q_norm_g, k_norm_g, attn_sinks,
           w_branch_a, w_branch_b, w_out, norm2_g, w_up, conv_w, conv_b, w_down):
    b, s, d = x.shape
    assert d == D_MODEL and w_in.shape == (1, D_MODEL, IN_COLS)
    m = b * s
    tm, ts = _tiles(s)

    og = jnp.tile(hgrn_out_g[0], HG_HEADS).reshape(1, HG_WIDTH)
    qg = jnp.tile(q_norm_g[0], ATT_HEADS).reshape(1, ATT_WIDTH)
    kg = jnp.tile(k_norm_g[0], ATT_KV_HEADS).reshape(1, ATT_KV_WIDTH)
    hq, hk, hlf, hv, hg, aq, ak, av, ga, gb = _inproj(
        x.reshape(m, d), norm1_g[0].reshape(1, d), w_in[0], hgrn_lb_logits, og, qg, kg,
        _block_diag_ones(2 * LANES, ATT_HD), _block_diag_ones(ATT_KV_WIDTH, ATT_HD), tm)

    r3 = lambda a: a.reshape(b, s, a.shape[-1])
    o_a, o_b = _seq(attn_sinks[0], r3(hq), r3(hk), r3(hlf), r3(hv), r3(hg), r3(aq), r3(ak), r3(av), ts)

    return _post(x, o_a, o_b, r3(ga), r3(gb),
                 w_branch_a[0].astype(BF16), w_branch_b[0].astype(BF16), w_out[0].astype(BF16),
                 norm2_g[0].reshape(1, d), w_up[0].astype(BF16),
                 conv_w[0], conv_b[0].reshape(1, D_FF), w_down[0].astype(BF16), tm)
```

```python
import functools

import numpy as np
import jax
import jax.numpy as jnp
from jax import lax
from jax.experimental import pallas as pl
from jax.experimental.pallas import tpu as pltpu

F32 = jnp.float32
BF16 = jnp.bfloat16

D_MODEL = 1024
HG_HEADS = 4
HG_DK = 128
HG_DV = 128
HG_WIDTH = HG_HEADS * HG_DK
HG_CHUNK = 32
ATT_HEADS = 8
ATT_KV_HEADS = 2
ATT_GROUP = ATT_HEADS // ATT_KV_HEADS
ATT_HD = 64
ATT_WIDTH = ATT_HEADS * ATT_HD
ATT_KV_WIDTH = ATT_KV_HEADS * ATT_HD
WINDOW = 128
D_FF = 2816
CONV_W = 3
EPS = 1e-6
NEG = -1e30
LOG2E = 1.4426950408889634

LANES = 128
SUBLANES = 8
BF16_ROWS = 2 * SUBLANES
V7X_VMEM_BYTES = 64 * 1024 * 1024
VMEM_LIMIT = V7X_VMEM_BYTES * 7 // 8
HG_TILE = 128
FF_CHUNK = 256
ROW_TILE = 512


def _tiles(s):
    assert s % ROW_TILE == 0 and ROW_TILE % HG_TILE == 0
    return ROW_TILE, 2 * ROW_TILE if s % (2 * ROW_TILE) == 0 else ROW_TILE

C_HQ, C_HF, C_HI, C_HG = 0, HG_WIDTH, 2 * HG_WIDTH, 3 * HG_WIDTH
C_AQ = 4 * HG_WIDTH
C_AKV = C_AQ + ATT_WIDTH
C_GA = C_AKV + 2 * ATT_KV_WIDTH
C_GB = C_GA + D_MODEL
IN_COLS = C_GB + D_MODEL


def _dot(a, b):
    return jnp.dot(a, b, preferred_element_type=F32)


def _dot_nt(a, b):
    return lax.dot_general(a, b, (((1,), (1,)), ((), ())), preferred_element_type=F32)


def _dot_tn(a, b):
    return lax.dot_general(a, b, (((0,), (0,)), ((), ())), preferred_element_type=F32)


def _sigmoid(x):
    return 0.5 * jnp.tanh(0.5 * x) + 0.5


def _rms(x, g):
    ms = jnp.mean(x * x, axis=-1, keepdims=True)
    return (x * lax.rsqrt(ms + EPS)) * g


def _inproj_kernel(x_ref, g1_ref, w_ref, lbl_ref, og_ref, qg_ref, kg_ref, bdq_ref, bdk_ref,
                   hq_ref, hk_ref, hlf_ref, hv_ref, hg_ref, aq_ref, ak_ref, av_ref, ga_ref, gb_ref, wb_ref):
    @pl.when(pl.program_id(0) == 0)
    def _():
        wb_ref[...] = w_ref[...].astype(BF16)

    h = _rms(x_ref[...], g1_ref[...]).astype(BF16)

    def proj(c0, width):
        return _dot(h, wb_ref[:, c0:c0 + width])

    lbl = lbl_ref[...]
    e = jnp.exp(lbl - jnp.max(lbl, axis=0, keepdims=True))
    lb = e[0:1] / jnp.sum(e, axis=0, keepdims=True)

    q_pre = proj(C_HQ, HG_WIDTH)
    hq_ref[...] = (q_pre * _sigmoid(q_pre)) * (HG_DK ** -0.5)
    f = lb + (1.0 - lb) * _sigmoid(proj(C_HF, HG_WIDTH))
    hk_ref[...] = 1.0 - f
    hlf_ref[...] = jnp.log2(f)
    g_pre = proj(C_HG, HG_WIDTH)
    hg_ref[...] = ((g_pre * _sigmoid(g_pre)) * og_ref[...]).astype(BF16)

    def gate(dst, c_w, part):
        w = D_MODEL // 2
        dst[:, part * w:(part + 1) * w] = _sigmoid(proj(c_w + part * w, w)).astype(BF16)

    assert C_AKV == C_AQ + ATT_WIDTH
    aqkv = proj(C_AQ, ATT_WIDTH + 2 * ATT_KV_WIDTH)
    a_q, kv = aqkv[:, :ATT_WIDTH], aqkv[:, ATT_WIDTH:]
    gate(ga_ref, C_GA, 0)
    sq = (a_q * a_q).astype(BF16)
    nbd = bdq_ref.shape[0]
    ms_q = jnp.concatenate([_dot(sq[:, c:c + nbd], bdq_ref[...]) for c in range(0, ATT_WIDTH, nbd)],
                           axis=1) * (1.0 / ATT_HD)
    aq_ref[...] = (((a_q * lax.rsqrt(ms_q + EPS)) * qg_ref[...]) * (LOG2E * ATT_HD ** -0.5)).astype(BF16)
    gate(ga_ref, C_GA, 1)
    a_k = kv[:, :ATT_KV_WIDTH]
    ms_k = _dot((a_k * a_k).astype(BF16), bdk_ref[...]) * (1.0 / ATT_HD)
    lo = lax.broadcasted_iota(jnp.int32, (1, LANES), 1) < ATT_HD
    for src, dst in (((a_k * lax.rsqrt(ms_k + EPS)) * kg_ref[...], ak_ref), (kv[:, ATT_KV_WIDTH:], av_ref)):
        swapped = pltpu.roll(src, ATT_HD, axis=1)
        dst[:, :LANES] = jnp.where(lo, src, swapped).astype(BF16)
        dst[:, LANES:] = jnp.where(lo, swapped, src).astype(BF16)
    gate(gb_ref, C_GB, 0)
    gate(gb_ref, C_GB, 1)
    hv_ref[...] = proj(C_HI, HG_WIDTH).astype(BF16)


def _inproj(x2, g1, w, lbl, og, qg, kg, bdq, bdk, tm):
    m = x2.shape[0]
    row = lambda c: pl.BlockSpec((tm, c), lambda i: (i, 0))
    full = lambda a: pl.BlockSpec(a.shape, lambda i: (0,) * a.ndim)
    outs = [
        (HG_WIDTH, F32), (HG_WIDTH, F32), (HG_WIDTH, F32), (HG_WIDTH, BF16), (HG_WIDTH, BF16),
        (ATT_WIDTH, BF16), (2 * LANES, BF16), (2 * LANES, BF16), (D_MODEL, BF16), (D_MODEL, BF16),
    ]
    return pl.pallas_call(
        _inproj_kernel,
        grid=(m // tm,),
        in_specs=[row(D_MODEL), full(g1),
                  pl.BlockSpec(w.shape, lambda i: (0, 0), pipeline_mode=pl.Buffered(1)),
                  full(lbl), full(og), full(qg), full(kg), full(bdq), full(bdk)],
        out_specs=[row(c) for c, _ in outs],
        out_shape=[jax.ShapeDtypeStruct((m, c), dt) for c, dt in outs],
        scratch_shapes=[pltpu.VMEM(w.shape, BF16)],
        compiler_params=pltpu.CompilerParams(dimension_semantics=("arbitrary",), vmem_limit_bytes=VMEM_LIMIT),
        name="inproj",
    )(x2, g1, w, lbl, og, qg, kg, bdq, bdk)


def _hgrn_head(q, k, v, g, sg, st, level):
    c = HG_CHUNK
    n = HG_TILE // c
    qs = [q[i * c:(i + 1) * c].astype(BF16) for i in range(n)]
    ks = [k[i * c:(i + 1) * c].astype(BF16) for i in range(n)]
    gs = [g[i * c:(i + 1) * c] for i in range(n)]
    mid = [gc[c // 2 - 1:c // 2] for gc in gs]
    end = [gc[c - 1:c] for gc in gs]
    p2 = lambda x: jnp.exp2(x).astype(BF16)
    cat = lambda parts: jnp.concatenate(parts, axis=0)
    zero = jnp.zeros((c, HG_DK), BF16)

    q1 = [qs[i] * p2(gs[i] - mid[i]) for i in range(n)]
    k1 = [ks[i] * p2(mid[i] - gs[i]) for i in range(n)]
    s1 = _dot_nt(cat(q1), cat(k1))
    q_moves = {(1, 0): mid[1] - end[0], (3, 2): mid[3] - end[2], (2, 1): mid[2] - end[1], (3, 1): mid[3] - end[1]}
    q_moves.update({(i, "s"): mid[i] for i in range(n)})
    k_moves = {(0, 0): end[0] - mid[0], (2, 2): end[2] - mid[2], (0, 1): end[1] - mid[0], (1, 1): end[1] - mid[1]}
    k_moves.update({(i, n - 1): end[n - 1] - mid[i] for i in range(n)})
    names = [("q",) + key for key in q_moves] + [("k",) + key for key in k_moves]
    stacked = jnp.exp2(jnp.concatenate(list(q_moves.values()) + list(k_moves.values()), axis=0))
    decay = {name: stacked[r:r + 1].astype(BF16) for r, name in enumerate(names)}
    q_to = lambda i, j: q1[i] * decay[("q", i, j)]
    k_to = lambda i, j: k1[i] * decay[("k", i, j)]
    s2 = _dot_nt(cat([zero, q_to(1, 0), zero, q_to(3, 2)]), cat([k_to(0, 0), zero, k_to(2, 2), zero]))
    s3 = _dot_nt(cat([zero, zero, q_to(2, 1), q_to(3, 1)]), cat([k_to(0, 1), k_to(1, 1), zero, zero]))
    scores = jnp.where(level == 1, s1, jnp.where(level == 2, s2, s3))
    o = _dot(scores.astype(BF16), v)

    k_in = cat([k_to(i, n - 1) for i in range(n)])
    q_in = cat([q_to(i, "s") for i in range(n)])
    o = o + _dot_nt(q_in, st.astype(BF16))
    st_new = st * jnp.exp2(end[3]) + _dot_tn(v, k_in)

    ms = jnp.mean(o * o, axis=-1, keepdims=True)
    return (o * lax.rsqrt(ms + EPS)) * sg, st_new


def _hgrn_prefix(i, lf_ref, tri_ref):
    lf = lf_ref[i * HG_TILE:(i + 1) * HG_TILE, :]
    lf_hi = lf.astype(BF16)
    lf_lo = (lf - lf_hi.astype(F32)).astype(BF16)
    return _dot(tri_ref[...], jnp.concatenate([lf_hi, lf_lo], axis=0))


def _hgrn_step(i, h, g_all, q_ref, k_ref, v_ref, sg_ref, level, o_ref, st_ref):
    rs = slice(i * HG_TILE, (i + 1) * HG_TILE)
    cs = slice(h * HG_DK, (h + 1) * HG_DK)
    o, st_new = _hgrn_head(q_ref[rs, cs], k_ref[rs, cs], v_ref[rs, cs], g_all[:, cs],
                           sg_ref[rs, cs].astype(F32), st_ref[h], level)
    st_ref[h] = st_new
    o_ref[rs, cs] = o.astype(BF16)


def _swa_step(i, j, first_tile, sink_ref, q_ref, kc_ref, kp_ref, vc_ref, vp_ref, bias_ref, o_ref):
    w = WINDOW
    lane = lax.broadcasted_iota(jnp.int32, (1, LANES), 1)
    lo_b = (lane < ATT_HD).astype(BF16)
    hi_b = (lane >= ATT_HD).astype(BF16)
    lo = lax.broadcasted_iota(jnp.int32, (w, LANES), 1) < ATT_HD
    row0 = lax.broadcasted_iota(jnp.int32, (SUBLANES, LANES), 0) == 0
    not_row0 = (lax.broadcasted_iota(jnp.int32, (BF16_ROWS, LANES), 0) > 0).astype(BF16)

    kvh = j // (ATT_GROUP // 2)
    cs = slice(kvh * LANES, (kvh + 1) * LANES)
    if i == 0:
        kk = jnp.concatenate([kp_ref[:, cs], kc_ref[0:w, cs]], axis=0)
        vv = jnp.concatenate([vp_ref[0:BF16_ROWS, cs] * not_row0, vp_ref[BF16_ROWS:, cs], vc_ref[0:w, cs]], axis=0)
    else:
        r0 = (i - 1) * w
        kk = kc_ref[r0:r0 + 2 * w, cs]
        vv = jnp.concatenate([vc_ref[r0:r0 + BF16_ROWS, cs] * not_row0,
                              vc_ref[r0 + BF16_ROWS:r0 + 2 * w, cs]], axis=0)
    qc = q_ref[i * w:(i + 1) * w, j * LANES:(j + 1) * LANES]
    qs = jnp.concatenate([qc * lo_b, qc * hi_b], axis=0)
    st = _dot_nt(kk, qs) + bias_ref[j]
    if i == 0:
        st = jnp.concatenate([st[:w] + jnp.where(first_tile, NEG, 0.0), st[w:]], axis=0)
    sink = jnp.concatenate([jnp.broadcast_to(sink_ref[2 * j + n] * LOG2E, (SUBLANES, LANES)) for n in range(2)],
                           axis=1)
    st = jnp.concatenate([jnp.where(jnp.concatenate([row0, row0], axis=1), sink, st[:SUBLANES]), st[SUBLANES:]],
                         axis=0)
    m = jnp.max(st, axis=0, keepdims=True)
    e = jnp.exp2(st - m)
    r = 1.0 / jnp.sum(e, axis=0, keepdims=True)
    o2 = _dot_tn(e.astype(BF16) * r.astype(BF16), vv)
    o_ref[i * w:(i + 1) * w, j * LANES:(j + 1) * LANES] = jnp.where(lo, o2[:w], o2[w:]).astype(BF16)


def _seq_kernel(sink_ref, hq_ref, hk_ref, hlf_ref, hv_ref, hg_ref, tri_ref, lvl_ref,
                aq_ref, kc_ref, kp_ref, vc_ref, vp_ref, bias_ref, oa_ref, ob_ref, st_ref, *, nsub):
    first_tile = pl.program_id(1) == 0

    @pl.when(first_tile)
    def _():
        st_ref[...] = jnp.zeros_like(st_ref)

    level = lvl_ref[...]
    assert HG_HEADS == ATT_HEADS // 2 and HG_TILE == WINDOW
    for i in range(nsub):
        g_all = _hgrn_prefix(i, hlf_ref, tri_ref)
        for h in range(HG_HEADS):
            _hgrn_step(i, h, g_all, hq_ref, hk_ref, hv_ref, hg_ref, level, oa_ref, st_ref)
            _swa_step(i, h, first_tile, sink_ref, aq_ref, kc_ref, kp_ref, vc_ref, vp_ref, bias_ref, ob_ref)


def _hgrn_consts():
    t = HG_TILE
    row = np.arange(t)[:, None]
    col = np.arange(t)[None, :]
    tri = (col <= row).astype(np.float32)
    ci, cj = row // HG_CHUNK, col // HG_CHUNK
    level = np.where((ci == cj) & (col <= row), 1, np.where((ci == cj + 1) & (cj % 2 == 0), 2, 0))
    return jnp.asarray(np.concatenate([tri, tri], axis=1), dtype=BF16), jnp.asarray(level, dtype=jnp.int32)


def _swa_bias():
    w = WINDOW
    i = np.arange(w)[:, None]
    j = np.arange(2 * w)[None, :]
    delta = (i + w - j).astype(np.float32)
    band = (delta >= 0) & (delta < w)
    assert not band[:, 0].any()
    slopes = 2.0 ** (-8.0 * np.arange(1, ATT_HEADS + 1, dtype=np.float32) / ATT_HEADS)
    per_head = np.where(band[None], -slopes[:, None, None] * delta[None] * np.float32(LOG2E), np.float32(NEG))
    pairs = per_head.astype(np.float32).reshape(ATT_HEADS // 2, 2 * w, 2 * w)
    return jnp.asarray(np.ascontiguousarray(pairs.transpose(0, 2, 1)))


def _seq(sinks, hq, hk, hlf, hv, hg, aq, ak, av, ts):
    b, s, _ = hq.shape
    tri, level = _hgrn_consts()
    bias = _swa_bias()
    nprev = ts // WINDOW
    wide = pl.BlockSpec((None, ts, HG_WIDTH), lambda bi, si, *_: (bi, si, 0))
    cur = pl.BlockSpec((None, ts, 2 * LANES), lambda bi, si, *_: (bi, si, 0))
    prev = pl.BlockSpec((None, WINDOW, 2 * LANES), lambda bi, si, *_: (bi, jnp.maximum(si * nprev - 1, 0), 0))
    full = lambda a: pl.BlockSpec(a.shape, lambda bi, si, *_: (0,) * a.ndim)
    return pl.pallas_call(
        functools.partial(_seq_kernel, nsub=ts // HG_TILE),
        grid_spec=pltpu.PrefetchScalarGridSpec(
            num_scalar_prefetch=1,
            grid=(b, s // ts),
            in_specs=[wide, wide, wide, wide, wide, full(tri), full(level),
                      wide, cur, prev, cur, prev, full(bias)],
            out_specs=[wide, wide],
            scratch_shapes=[pltpu.VMEM((HG_HEADS, HG_DV, HG_DK), F32)],
        ),
        out_shape=[jax.ShapeDtypeStruct((b, s, HG_WIDTH), BF16), jax.ShapeDtypeStruct((b, s, ATT_WIDTH), BF16)],
        compiler_params=pltpu.CompilerParams(dimension_semantics=("arbitrary", "arbitrary"),
                                             vmem_limit_bytes=VMEM_LIMIT),
        name="seq",
    )(sinks, hq, hk, hlf, hv, hg, tri, level, aq, ak, ak, av, av, bias)


def _post_kernel(x_ref, oa_ref, ob_ref, ga_ref, gb_ref, wa_ref, wb_ref, wo_ref, g2_ref,
                 wu_ref, cw_ref, cb_ref, wd_ref, o_ref, x1_ref, h2_ref, gs_ref, carry_ref, act_ref, *, tm):
    pad = SUBLANES

    @pl.when(pl.program_id(1) == 0)
    def _():
        carry_ref[...] = jnp.zeros_like(carry_ref)

    mixed = (ga_ref[...].astype(F32) * _dot(oa_ref[...], wa_ref[...])
             + gb_ref[...].astype(F32) * _dot(ob_ref[...], wb_ref[...]))
    x1_ref[...] = x_ref[...] + _dot(mixed.astype(BF16), wo_ref[...])
    h2_ref[...] = _rms(x1_ref[...], g2_ref[...]).astype(BF16)

    h2 = h2_ref[...]
    for c0 in range(0, D_FF, FF_CHUNK):
        cw = min(FF_CHUNK, D_FF - c0)
        cs = slice(c0, c0 + cw)
        gate = _dot(h2, wu_ref[:, cs])
        val = _dot(h2, wu_ref[:, D_FF + c0:D_FF + c0 + cw])
        gs_ref[0:pad, :cw] = carry_ref[:, cs]
        gs_ref[pad:pad + tm, :cw] = gate
        carry_ref[:, cs] = gate[tm - pad:tm, :]
        conv = cb_ref[:, cs] + cw_ref[2:3, cs] * gate
        conv = conv + cw_ref[1:2, cs] * gs_ref[pad - 1:pad - 1 + tm, :cw]
        conv = conv + cw_ref[0:1, cs] * gs_ref[pad - 2:pad - 2 + tm, :cw]
        gelu = (0.5 * conv) * (1.0 + lax.erf(conv * (0.5 ** 0.5)))
        act_ref[:, cs] = (gelu * val).astype(BF16)
    o_ref[...] = x1_ref[...] + _dot(act_ref[...], wd_ref[...])


def _post(x, oa, ob, ga, gb, wa, wb, wo, g2, wu, cw, cb, wd, tm):
    b, s, _ = x.shape
    row = lambda c: pl.BlockSpec((None, tm, c), lambda bi, si: (bi, si, 0))
    full = lambda a: pl.BlockSpec(a.shape, lambda bi, si: (0,) * a.ndim, pipeline_mode=pl.Buffered(1))
    return pl.pallas_call(
        functools.partial(_post_kernel, tm=tm),
        grid=(b, s // tm),
        in_specs=[row(D_MODEL), row(HG_WIDTH), row(ATT_WIDTH), row(D_MODEL), row(D_MODEL),
                  full(wa), full(wb), full(wo), full(g2), full(wu), full(cw), full(cb), full(wd)],
        out_specs=row(D_MODEL),
        out_shape=jax.ShapeDtypeStruct((b, s, D_MODEL), F32),
        scratch_shapes=[pltpu.VMEM((tm, D_MODEL), F32),
                        pltpu.VMEM((tm, D_MODEL), BF16),
                        pltpu.VMEM((tm + SUBLANES, FF_CHUNK), F32),
                        pltpu.VMEM((SUBLANES, D_FF), F32),
                        pltpu.VMEM((tm, D_FF), BF16)],
        compiler_params=pltpu.CompilerParams(dimension_semantics=("arbitrary", "arbitrary"),
                                             vmem_limit_bytes=VMEM_LIMIT),
        name="post",
    )(x, oa, ob, ga, gb, wa, wb, wo, g2, wu, cw, cb, wd)


def _block_diag_ones(n, blk):
    idx = np.arange(n) // blk
    return jnp.asarray((idx[:, None] == idx[None, :]).astype(np.float32), dtype=BF16)


def kernel(x, norm1_g, w_in, hgrn_lb_logits, hgrn_out_g, q_norm_g, k_norm_g, attn_sinks,
           w_branch_a, w_branch_b, w_out, norm2_g, w_up, conv_w, conv_b, w_down):
    b, s, d = x.shape
    assert d == D_MODEL and w_in.shape == (1, D_MODEL, IN_COLS)
    m = b * s
    tm, ts = _tiles(s)

    og = jnp.tile(hgrn_out_g[0], HG_HEADS).reshape(1, HG_WIDTH)
    qg = jnp.tile(q_norm_g[0], ATT_HEADS).reshape(1, ATT_WIDTH)
    kg = jnp.tile(k_norm_g[0], ATT_KV_HEADS).reshape(1, ATT_KV_WIDTH)
    hq, hk, hlf, hv, hg, aq, ak, av, ga, gb = _inproj(
        x.reshape(m, d), norm1_g[0].reshape(1, d), w_in[0], hgrn_lb_logits, og, qg, kg,
        _block_diag_ones(2 * LANES, ATT_HD), _block_diag_ones(ATT_KV_WIDTH, ATT_HD), tm)

    r3 = lambda a: a.reshape(b, s, a.shape[-1])
    o_a, o_b = _seq(attn_sinks[0], r3(hq), r3(hk), r3(hlf), r3(hv), r3(hg), r3(aq), r3(ak), r3(av), ts)

    return _post(x, o_a, o_b, r3(ga), r3(gb),
                 w_branch_a[0].astype(BF16), w_branch_b[0].astype(BF16), w_out[0].astype(BF16),
                 norm2_g[0].reshape(1, d), w_up[0].astype(BF16),
                 conv_w[0], conv_b[0].reshape(1, D_FF), w_down[0].astype(BF16), tm)
```

```python
import functools

import numpy as np
import jax
import jax.numpy as jnp
from jax import lax
from jax.experimental import pallas as pl
from jax.experimental.pallas import tpu as pltpu

F32 = jnp.float32
BF16 = jnp.bfloat16

D_MODEL = 1024
HG_HEADS = 4
HG_DK = 128
HG_DV = 128
HG_WIDTH = HG_HEADS * HG_DK
HG_CHUNK = 32
ATT_HEADS = 8
ATT_KV_HEADS = 2
ATT_GROUP = ATT_HEADS // ATT_KV_HEADS
ATT_HD = 64
ATT_WIDTH = ATT_HEADS * ATT_HD
ATT_KV_WIDTH = ATT_KV_HEADS * ATT_HD
WINDOW = 128
D_FF = 2816
CONV_W = 3
EPS = 1e-6
NEG = -1e30
LOG2E = 1.4426950408889634

LANES = 128
SUBLANES = 8
BF16_ROWS = 2 * SUBLANES
V7X_VMEM_BYTES = 64 * 1024 * 1024
VMEM_LIMIT = V7X_VMEM_BYTES * 7 // 8
HG_TILE = 128
FF_CHUNK = 256
ROW_TILE = 512


def _tiles(s):
    assert s % ROW_TILE == 0 and ROW_TILE % HG_TILE == 0
    return ROW_TILE, 2 * ROW_TILE if s % (2 * ROW_TILE) == 0 else ROW_TILE

C_HQ, C_HF, C_HI, C_HG = 0, HG_WIDTH, 2 * HG_WIDTH, 3 * HG_WIDTH
C_AQ = 4 * HG_WIDTH
C_AKV = C_AQ + ATT_WIDTH
C_GA = C_AKV + 2 * ATT_KV_WIDTH
C_GB = C_GA + D_MODEL
IN_COLS = C_GB + D_MODEL


def _dot(a, b):
    return jnp.dot(a, b, preferred_element_type=F32)


def _dot_nt(a, b):
    return lax.dot_general(a, b, (((1,), (1,)), ((), ())), preferred_element_type=F32)


def _dot_tn(a, b):
    return lax.dot_general(a, b, (((0,), (0,)), ((), ())), preferred_element_type=F32)


def _sigmoid(x):
    return 0.5 * jnp.tanh(0.5 * x) + 0.5


def _rms(x, g):
    ms = jnp.mean(x * x, axis=-1, keepdims=True)
    return (x * lax.rsqrt(ms + EPS)) * g


def _inproj_kernel(x_ref, g1_ref, w_ref, lbl_ref, og_ref, qg_ref, kg_ref, bdq_ref, bdk_ref,
                   hq_ref, hk_ref, hlf_ref, hv_ref, hg_ref, aq_ref, ak_ref, av_ref, ga_ref, gb_ref, wb_ref):
    @pl.when(pl.program_id(0) == 0)
    def _():
        wb_ref[...] = w_ref[...].astype(BF16)

    h = _rms(x_ref[...], g1_ref[...]).astype(BF16)

    def proj(c0, width):
        return _dot(h, wb_ref[:, c0:c0 + width])

    lbl = lbl_ref[...]
    e = jnp.exp(lbl - jnp.max(lbl, axis=0, keepdims=True))
    lb = e[0:1] / jnp.sum(e, axis=0, keepdims=True)

    q_pre = proj(C_HQ, HG_WIDTH)
    hq_ref[...] = (q_pre * _sigmoid(q_pre)) * (HG_DK ** -0.5)
    f = lb + (1.0 - lb) * _sigmoid(proj(C_HF, HG_WIDTH))
    hk_ref[...] = 1.0 - f
    hlf_ref[...] = jnp.log2(f)
    g_pre = proj(C_HG, HG_WIDTH)
    hg_ref[...] = ((g_pre * _sigmoid(g_pre)) * og_ref[...]).astype(BF16)

    def gate(dst, c_w, part):
        w = D_MODEL // 2
        dst[:, part * w:(part + 1) * w] = proj(c_w + part * w, w).astype(BF16)

    assert C_AKV == C_AQ + ATT_WIDTH
    aqkv = proj(C_AQ, ATT_WIDTH + 2 * ATT_KV_WIDTH)
    a_q, kv = aqkv[:, :ATT_WIDTH], aqkv[:, ATT_WIDTH:]
    gate(ga_ref, C_GA, 0)
    sq = (a_q * a_q).astype(BF16)
    nbd = bdq_ref.shape[0]
    ms_q = jnp.concatenate([_dot(sq[:, c:c + nbd], bdq_ref[...]) for c in range(0, ATT_WIDTH, nbd)],
                           axis=1) * (1.0 / ATT_HD)
    aq_ref[...] = (((a_q * lax.rsqrt(ms_q + EPS)) * qg_ref[...]) * (LOG2E * ATT_HD ** -0.5)).astype(BF16)
    gate(ga_ref, C_GA, 1)
    a_k = kv[:, :ATT_KV_WIDTH]
    ms_k = _dot((a_k * a_k).astype(BF16), bdk_ref[...]) * (1.0 / ATT_HD)
    lo = lax.broadcasted_iota(jnp.int32, (1, LANES), 1) < ATT_HD
    for src, dst in (((a_k * lax.rsqrt(ms_k + EPS)) * kg_ref[...], ak_ref), (kv[:, ATT_KV_WIDTH:], av_ref)):
        swapped = pltpu.roll(src, ATT_HD, axis=1)
        dst[:, :LANES] = jnp.where(lo, src, swapped).astype(BF16)
        dst[:, LANES:] = jnp.where(lo, swapped, src).astype(BF16)
    gate(gb_ref, C_GB, 0)
    gate(gb_ref, C_GB, 1)
    hv_ref[...] = proj(C_HI, HG_WIDTH).astype(BF16)


def _inproj(x2, g1, w, lbl, og, qg, kg, bdq, bdk, tm):
    m = x2.shape[0]
    row = lambda c: pl.BlockSpec((tm, c), lambda i: (i, 0))
    full = lambda a: pl.BlockSpec(a.shape, lambda i: (0,) * a.ndim)
    outs = [
        (HG_WIDTH, F32), (HG_WIDTH, F32), (HG_WIDTH, F32), (HG_WIDTH, BF16), (HG_WIDTH, BF16),
        (ATT_WIDTH, BF16), (2 * LANES, BF16), (2 * LANES, BF16), (D_MODEL, BF16), (D_MODEL, BF16),
    ]
    return pl.pallas_call(
        _inproj_kernel,
        grid=(m // tm,),
        in_specs=[row(D_MODEL), full(g1),
                  pl.BlockSpec(w.shape, lambda i: (0, 0), pipeline_mode=pl.Buffered(1)),
                  full(lbl), full(og), full(qg), full(kg), full(bdq), full(bdk)],
        out_specs=[row(c) for c, _ in outs],
        out_shape=[jax.ShapeDtypeStruct((m, c), dt) for c, dt in outs],
        scratch_shapes=[pltpu.VMEM(w.shape, BF16)],
        compiler_params=pltpu.CompilerParams(dimension_semantics=("arbitrary",), vmem_limit_bytes=VMEM_LIMIT),
        name="inproj",
    )(x2, g1, w, lbl, og, qg, kg, bdq, bdk)


def _hgrn_head(q, k, v, g, sg, st, level):
    c = HG_CHUNK
    n = HG_TILE // c
    qs = [q[i * c:(i + 1) * c].astype(BF16) for i in range(n)]
    ks = [k[i * c:(i + 1) * c].astype(BF16) for i in range(n)]
    gs = [g[i * c:(i + 1) * c] for i in range(n)]
    mid = [gc[c // 2 - 1:c // 2] for gc in gs]
    end = [gc[c - 1:c] for gc in gs]
    p2 = lambda x: jnp.exp2(x).astype(BF16)
    cat = lambda parts: jnp.concatenate(parts, axis=0)
    zero = jnp.zeros((c, HG_DK), BF16)

    q1 = [qs[i] * p2(gs[i] - mid[i]) for i in range(n)]
    k1 = [ks[i] * p2(mid[i] - gs[i]) for i in range(n)]
    s1 = _dot_nt(cat(q1), cat(k1))
    q_moves = {(1, 0): mid[1] - end[0], (3, 2): mid[3] - end[2], (2, 1): mid[2] - end[1], (3, 1): mid[3] - end[1]}
    q_moves.update({(i, "s"): mid[i] for i in range(n)})
    k_moves = {(0, 0): end[0] - mid[0], (2, 2): end[2] - mid[2], (0, 1): end[1] - mid[0], (1, 1): end[1] - mid[1]}
    k_moves.update({(i, n - 1): end[n - 1] - mid[i] for i in range(n)})
    names = [("q",) + key for key in q_moves] + [("k",) + key for key in k_moves]
    stacked = jnp.exp2(jnp.concatenate(list(q_moves.values()) + list(k_moves.values()), axis=0))
    decay = {name: stacked[r:r + 1].astype(BF16) for r, name in enumerate(names)}
    q_to = lambda i, j: q1[i] * decay[("q", i, j)]
    k_to = lambda i, j: k1[i] * decay[("k", i, j)]
    s2 = _dot_nt(cat([zero, q_to(1, 0), zero, q_to(3, 2)]), cat([k_to(0, 0), zero, k_to(2, 2), zero]))
    s3 = _dot_nt(cat([zero, zero, q_to(2, 1), q_to(3, 1)]), cat([k_to(0, 1), k_to(1, 1), zero, zero]))
    scores = jnp.where(level == 1, s1, jnp.where(level == 2, s2, s3))
    o = _dot(scores.astype(BF16), v)

    k_in = cat([k_to(i, n - 1) for i in range(n)])
    q_in = cat([q_to(i, "s") for i in range(n)])
    o = o + _dot_nt(q_in, st.astype(BF16))
    st_new = st * jnp.exp2(end[3]) + _dot_tn(v, k_in)

    ms = jnp.mean(o * o, axis=-1, keepdims=True)
    return (o * lax.rsqrt(ms + EPS)) * sg, st_new


def _hgrn_prefix(i, lf_ref, tri_ref):
    lf = lf_ref[i * HG_TILE:(i + 1) * HG_TILE, :]
    lf_hi = lf.astype(BF16)
    lf_lo = (lf - lf_hi.astype(F32)).astype(BF16)
    return _dot(tri_ref[...], jnp.concatenate([lf_hi, lf_lo], axis=0))


def _hgrn_step(i, h, g_all, q_ref, k_ref, v_ref, sg_ref, level, o_ref, st_ref):
    rs = slice(i * HG_TILE, (i + 1) * HG_TILE)
    cs = slice(h * HG_DK, (h + 1) * HG_DK)
    o, st_new = _hgrn_head(q_ref[rs, cs], k_ref[rs, cs], v_ref[rs, cs], g_all[:, cs],
                           sg_ref[rs, cs].astype(F32), st_ref[h], level)
    st_ref[h] = st_new
    o_ref[rs, cs] = o.astype(BF16)


def _swa_step(i, j, first_tile, sink_ref, q_ref, kc_ref, kp_ref, vc_ref, vp_ref, bias_ref, o_ref):
    w = WINDOW
    lane = lax.broadcasted_iota(jnp.int32, (1, LANES), 1)
    lo_b = (lane < ATT_HD).astype(BF16)
    hi_b = (lane >= ATT_HD).astype(BF16)
    lo = lax.broadcasted_iota(jnp.int32, (w, LANES), 1) < ATT_HD
    row0 = lax.broadcasted_iota(jnp.int32, (SUBLANES, LANES), 0) == 0
    not_row0 = (lax.broadcasted_iota(jnp.int32, (BF16_ROWS, LANES), 0) > 0).astype(BF16)

    kvh = j // (ATT_GROUP // 2)
    cs = slice(kvh * LANES, (kvh + 1) * LANES)
    if i == 0:
        kk = jnp.concatenate([kp_ref[:, cs], kc_ref[0:w, cs]], axis=0)
        vv = jnp.concatenate([vp_ref[0:BF16_ROWS, cs] * not_row0, vp_ref[BF16_ROWS:, cs], vc_ref[0:w, cs]], axis=0)
    else:
        r0 = (i - 1) * w
        kk = kc_ref[r0:r0 + 2 * w, cs]
        vv = jnp.concatenate([vc_ref[r0:r0 + BF16_ROWS, cs] * not_row0,
                              vc_ref[r0 + BF16_ROWS:r0 + 2 * w, cs]], axis=0)
    qc = q_ref[i * w:(i + 1) * w, j * LANES:(j + 1) * LANES]
    qs = jnp.concatenate([qc * lo_b, qc * hi_b], axis=0)
    st = _dot_nt(kk, qs) + bias_ref[j]
    if i == 0:
        st = jnp.concatenate([st[:w] + jnp.where(first_tile, NEG, 0.0), st[w:]], axis=0)
    sink = jnp.concatenate([jnp.broadcast_to(sink_ref[2 * j + n] * LOG2E, (SUBLANES, LANES)) for n in range(2)],
                           axis=1)
    st = jnp.concatenate([jnp.where(jnp.concatenate([row0, row0], axis=1), sink, st[:SUBLANES]), st[SUBLANES:]],
                         axis=0)
    m = jnp.max(st, axis=0, keepdims=True)
    e = jnp.exp2(st - m)
    r = 1.0 / jnp.sum(e, axis=0, keepdims=True)
    o2 = _dot_tn(e.astype(BF16) * r.astype(BF16), vv)
    o_ref[i * w:(i + 1) * w, j * LANES:(j + 1) * LANES] = jnp.where(lo, o2[:w], o2[w:]).astype(BF16)


def _seq_kernel(sink_ref, hq_ref, hk_ref, hlf_ref, hv_ref, hg_ref, tri_ref, lvl_ref,
                aq_ref, kc_ref, kp_ref, vc_ref, vp_ref, bias_ref, oa_ref, ob_ref, st_ref, *, nsub):
    first_tile = pl.program_id(1) == 0

    @pl.when(first_tile)
    def _():
        st_ref[...] = jnp.zeros_like(st_ref)

    level = lvl_ref[...]
    assert HG_HEADS == ATT_HEADS // 2 and HG_TILE == WINDOW
    for i in range(nsub):
        g_all = _hgrn_prefix(i, hlf_ref, tri_ref)
        for h in range(HG_HEADS):
            _hgrn_step(i, h, g_all, hq_ref, hk_ref, hv_ref, hg_ref, level, oa_ref, st_ref)
            _swa_step(i, h, first_tile, sink_ref, aq_ref, kc_ref, kp_ref, vc_ref, vp_ref, bias_ref, ob_ref)


def _hgrn_consts():
    t = HG_TILE
    row = np.arange(t)[:, None]
    col = np.arange(t)[None, :]
    tri = (col <= row).astype(np.float32)
    ci, cj = row // HG_CHUNK, col // HG_CHUNK
    level = np.where((ci == cj) & (col <= row), 1, np.where((ci == cj + 1) & (cj % 2 == 0), 2, 0))
    return jnp.asarray(np.concatenate([tri, tri], axis=1), dtype=BF16), jnp.asarray(level, dtype=jnp.int32)


def _swa_bias():
    w = WINDOW
    i = np.arange(w)[:, None]
    j = np.arange(2 * w)[None, :]
    delta = (i + w - j).astype(np.float32)
    band = (delta >= 0) & (delta < w)
    assert not band[:, 0].any()
    slopes = 2.0 ** (-8.0 * np.arange(1, ATT_HEADS + 1, dtype=np.float32) / ATT_HEADS)
    per_head = np.where(band[None], -slopes[:, None, None] * delta[None] * np.float32(LOG2E), np.float32(NEG))
    pairs = per_head.astype(np.float32).reshape(ATT_HEADS // 2, 2 * w, 2 * w)
    return jnp.asarray(np.ascontiguousarray(pairs.transpose(0, 2, 1)))


def _seq(sinks, hq, hk, hlf, hv, hg, aq, ak, av, ts):
    b, s, _ = hq.shape
    tri, level = _hgrn_consts()
    bias = _swa_bias()
    nprev = ts // WINDOW
    wide = pl.BlockSpec((None, ts, HG_WIDTH), lambda bi, si, *_: (bi, si, 0))
    cur = pl.BlockSpec((None, ts, 2 * LANES), lambda bi, si, *_: (bi, si, 0))
    prev = pl.BlockSpec((None, WINDOW, 2 * LANES), lambda bi, si, *_: (bi, jnp.maximum(si * nprev - 1, 0), 0))
    full = lambda a: pl.BlockSpec(a.shape, lambda bi, si, *_: (0,) * a.ndim)
    return pl.pallas_call(
        functools.partial(_seq_kernel, nsub=ts // HG_TILE),
        grid_spec=pltpu.PrefetchScalarGridSpec(
            num_scalar_prefetch=1,
            grid=(b, s // ts),
            in_specs=[wide, wide, wide, wide, wide, full(tri), full(level),
                      wide, cur, prev, cur, prev, full(bias)],
            out_specs=[wide, wide],
            scratch_shapes=[pltpu.VMEM((HG_HEADS, HG_DV, HG_DK), F32)],
        ),
        out_shape=[jax.ShapeDtypeStruct((b, s, HG_WIDTH), BF16), jax.ShapeDtypeStruct((b, s, ATT_WIDTH), BF16)],
        compiler_params=pltpu.CompilerParams(dimension_semantics=("arbitrary", "arbitrary"),
                                             vmem_limit_bytes=VMEM_LIMIT),
        name="seq",
    )(sinks, hq, hk, hlf, hv, hg, tri, level, aq, ak, ak, av, av, bias)


def _post_kernel(x_ref, oa_ref, ob_ref, ga_ref, gb_ref, wa_ref, wb_ref, wo_ref, g2_ref,
                 wu_ref, cw_ref, cb_ref, wd_ref, o_ref, x1_ref, h2_ref, gs_ref, carry_ref, act_ref, *, tm):
    pad = SUBLANES

    @pl.when(pl.program_id(1) == 0)
    def _():
        carry_ref[...] = jnp.zeros_like(carry_ref)

    mixed = (_sigmoid(ga_ref[...].astype(F32)) * _dot(oa_ref[...], wa_ref[...])
             + _sigmoid(gb_ref[...].astype(F32)) * _dot(ob_ref[...], wb_ref[...]))
    x1_ref[...] = x_ref[...] + _dot(mixed.astype(BF16), wo_ref[...])
    h2_ref[...] = _rms(x1_ref[...], g2_ref[...]).astype(BF16)

    h2 = h2_ref[...]
    for c0 in range(0, D_FF, FF_CHUNK):
        cw = min(FF_CHUNK, D_FF - c0)
        cs = slice(c0, c0 + cw)
        gate = _dot(h2, wu_ref[:, cs])
        val = _dot(h2, wu_ref[:, D_FF + c0:D_FF + c0 + cw])
        gs_ref[0:pad, :cw] = carry_ref[:, cs]
        gs_ref[pad:pad + tm, :cw] = gate
        carry_ref[:, cs] = gate[tm - pad:tm, :]
        conv = cb_ref[:, cs] + cw_ref[2:3, cs] * gate
        conv = conv + cw_ref[1:2, cs] * gs_ref[pad - 1:pad - 1 + tm, :cw]
        conv = conv + cw_ref[0:1, cs] * gs_ref[pad - 2:pad - 2 + tm, :cw]
        gelu = (0.5 * conv) * (1.0 + lax.erf(conv * (0.5 ** 0.5)))
        act_ref[:, cs] = (gelu * val).astype(BF16)
    o_ref[...] = x1_ref[...] + _dot(act_ref[...], wd_ref[...])


def _post(x, oa, ob, ga, gb, wa, wb, wo, g2, wu, cw, cb, wd, tm):
    b, s, _ = x.shape
    row = lambda c: pl.BlockSpec((None, tm, c), lambda bi, si: (bi, si, 0))
    full = lambda a: pl.BlockSpec(a.shape, lambda bi, si: (0,) * a.ndim, pipeline_mode=pl.Buffered(1))
    return pl.pallas_call(
        functools.partial(_post_kernel, tm=tm),
        grid=(b, s // tm),
        in_specs=[row(D_MODEL), row(HG_WIDTH), row(ATT_WIDTH), row(D_MODEL), row(D_MODEL),
                  full(wa), full(wb), full(wo), full(g2), full(wu), full(cw), full(cb), full(wd)],
        out_specs=row(D_MODEL),
        out_shape=jax.ShapeDtypeStruct((b, s, D_MODEL), F32),
        scratch_shapes=[pltpu.VMEM((tm, D_MODEL), F32),
                        pltpu.VMEM((tm, D_MODEL), BF16),
                        pltpu.VMEM((tm + SUBLANES, FF_CHUNK), F32),
                        pltpu.VMEM((SUBLANES, D_FF), F32),
                        pltpu.VMEM((tm, D_FF), BF16)],
        compiler_params=pltpu.CompilerParams(dimension_semantics=("arbitrary", "arbitrary"),
                                             vmem_limit_bytes=VMEM_LIMIT),
        name="post",
    )(x, oa, ob, ga, gb, wa, wb, wo, g2, wu, cw, cb, wd)


def _block_diag_ones(n, blk):
    idx = np.arange(n) // blk
    return jnp.asarray((idx[:, None] == idx[None, :]).astype(np.float32), dtype=BF16)


def kernel(x, norm1_g, w_in, hgrn_lb_logits, hgrn_out_g, q_norm_g, k_norm_g, attn_sinks,
           w_branch_a, w_branch_b, w_out, norm2_g, w_up, conv_w, conv_b, w_down):
    b, s, d = x.shape
    assert d == D_MODEL and w_in.shape == (1, D_MODEL, IN_COLS)
    m = b * s
    tm, ts = _tiles(s)

    og = jnp.tile(hgrn_out_g[0], HG_HEADS).reshape(1, HG_WIDTH)
    qg = jnp.tile(q_norm_g[0], ATT_HEADS).reshape(1, ATT_WIDTH)
    kg = jnp.tile(k_norm_g[0], ATT_KV_HEADS).reshape(1, ATT_KV_WIDTH)
    hq, hk, hlf, hv, hg, aq, ak, av, ga, gb = _inproj(
        x.reshape(m, d), norm1_g[0].reshape(1, d), w_in[0], hgrn_lb_logits, og, qg, kg,
        _block_diag_ones(2 * LANES, ATT_HD), _block_diag_ones(ATT_KV_WIDTH, ATT_HD), tm)

    r3 = lambda a: a.reshape(b, s, a.shape[-1])
    o_a, o_b = _seq(attn_sinks[0], r3(hq), r3(hk), r3(hlf), r3(hv), r3(hg), r3(aq), r3(ak), r3(av), ts)

    return _post(x, o_a, o_b, r3(ga), r3(gb),
                 w_branch_a[0].astype(BF16), w_branch_b[0].astype(BF16), w_out[0].astype(BF16),
                 norm2_g[0].reshape(1, d), w_up[0].astype(BF16),
                 conv_w[0], conv_b[0].reshape(1, D_FF), w_down[0].astype(BF16), tm)
```

```python
import functools

import numpy as np
import jax
import jax.numpy as jnp
from jax import lax
from jax.experimental import pallas as pl
from jax.experimental.pallas import tpu as pltpu

F32 = jnp.float32
BF16 = jnp.bfloat16

D_MODEL = 1024
HG_HEADS = 4
HG_DK = 128
HG_DV = 128
HG_WIDTH = HG_HEADS * HG_DK
HG_CHUNK = 32
ATT_HEADS = 8
ATT_KV_HEADS = 2
ATT_GROUP = ATT_HEADS // ATT_KV_HEADS
ATT_HD = 64
ATT_WIDTH = ATT_HEADS * ATT_HD
ATT_KV_WIDTH = ATT_KV_HEADS * ATT_HD
WINDOW = 128
D_FF = 2816
CONV_W = 3
EPS = 1e-6
NEG = -1e30
LOG2E = 1.4426950408889634

LANES = 128
SUBLANES = 8
BF16_ROWS = 2 * SUBLANES
V7X_VMEM_BYTES = 64 * 1024 * 1024
VMEM_LIMIT = V7X_VMEM_BYTES * 7 // 8
HG_TILE = 128
FF_CHUNK = 256
ROW_TILE = 512


def _tiles(s):
    assert s % ROW_TILE == 0 and ROW_TILE % HG_TILE == 0
    return ROW_TILE, next(k * ROW_TILE for k in (4, 2, 1) if s % (k * ROW_TILE) == 0)

C_HQ, C_HF, C_HI, C_HG = 0, HG_WIDTH, 2 * HG_WIDTH, 3 * HG_WIDTH
C_AQ = 4 * HG_WIDTH
C_AKV = C_AQ + ATT_WIDTH
C_GA = C_AKV + 2 * ATT_KV_WIDTH
C_GB = C_GA + D_MODEL
IN_COLS = C_GB + D_MODEL


def _dot(a, b):
    return jnp.dot(a, b, preferred_element_type=F32)


def _dot_nt(a, b):
    return lax.dot_general(a, b, (((1,), (1,)), ((), ())), preferred_element_type=F32)


def _dot_tn(a, b):
    return lax.dot_general(a, b, (((0,), (0,)), ((), ())), preferred_element_type=F32)


def _sigmoid(x):
    return 0.5 * jnp.tanh(0.5 * x) + 0.5


def _rms(x, g):
    ms = jnp.mean(x * x, axis=-1, keepdims=True)
    return (x * lax.rsqrt(ms + EPS)) * g


def _inproj_kernel(x_ref, g1_ref, w_ref, lbl_ref, og_ref, qg_ref, kg_ref, bdq_ref, bdk_ref,
                   hq_ref, hk_ref, hlf_ref, hv_ref, hg_ref, aq_ref, ak_ref, av_ref, ga_ref, gb_ref, wb_ref):
    @pl.when(pl.program_id(0) == 0)
    def _():
        wb_ref[...] = w_ref[...].astype(BF16)

    h = _rms(x_ref[...], g1_ref[...]).astype(BF16)

    def proj(c0, width):
        return _dot(h, wb_ref[:, c0:c0 + width])

    lbl = lbl_ref[...]
    e = jnp.exp(lbl - jnp.max(lbl, axis=0, keepdims=True))
    lb = e[0:1] / jnp.sum(e, axis=0, keepdims=True)

    q_pre = proj(C_HQ, HG_WIDTH)
    hq_ref[...] = (q_pre * _sigmoid(q_pre)) * (HG_DK ** -0.5)
    f = lb + (1.0 - lb) * _sigmoid(proj(C_HF, HG_WIDTH))
    hk_ref[...] = 1.0 - f
    hlf_ref[...] = jnp.log2(f)
    g_pre = proj(C_HG, HG_WIDTH)
    hg_ref[...] = ((g_pre * _sigmoid(g_pre)) * og_ref[...]).astype(BF16)

    def gate(dst, c_w, part):
        w = D_MODEL // 2
        dst[:, part * w:(part + 1) * w] = _sigmoid(proj(c_w + part * w, w)).astype(BF16)

    assert C_AKV == C_AQ + ATT_WIDTH
    aqkv = proj(C_AQ, ATT_WIDTH + 2 * ATT_KV_WIDTH)
    a_q, kv = aqkv[:, :ATT_WIDTH], aqkv[:, ATT_WIDTH:]
    gate(ga_ref, C_GA, 0)
    sq = (a_q * a_q).astype(BF16)
    nbd = bdq_ref.shape[0]
    ms_q = jnp.concatenate([_dot(sq[:, c:c + nbd], bdq_ref[...]) for c in range(0, ATT_WIDTH, nbd)],
                           axis=1) * (1.0 / ATT_HD)
    aq_ref[...] = (((a_q * lax.rsqrt(ms_q + EPS)) * qg_ref[...]) * (LOG2E * ATT_HD ** -0.5)).astype(BF16)
    gate(ga_ref, C_GA, 1)
    a_k = kv[:, :ATT_KV_WIDTH]
    ms_k = _dot((a_k * a_k).astype(BF16), bdk_ref[...]) * (1.0 / ATT_HD)
    lo = lax.broadcasted_iota(jnp.int32, (1, LANES), 1) < ATT_HD
    for src, dst in (((a_k * lax.rsqrt(ms_k + EPS)) * kg_ref[...], ak_ref), (kv[:, ATT_KV_WIDTH:], av_ref)):
        swapped = pltpu.roll(src, ATT_HD, axis=1)
        dst[:, :LANES] = jnp.where(lo, src, swapped).astype(BF16)
        dst[:, LANES:] = jnp.where(lo, swapped, src).astype(BF16)
    gate(gb_ref, C_GB, 0)
    gate(gb_ref, C_GB, 1)
    hv_ref[...] = proj(C_HI, HG_WIDTH).astype(BF16)


def _inproj(x2, g1, w, lbl, og, qg, kg, bdq, bdk, tm):
    m = x2.shape[0]
    row = lambda c: pl.BlockSpec((tm, c), lambda i: (i, 0))
    full = lambda a: pl.BlockSpec(a.shape, lambda i: (0,) * a.ndim)
    outs = [
        (HG_WIDTH, F32), (HG_WIDTH, F32), (HG_WIDTH, F32), (HG_WIDTH, BF16), (HG_WIDTH, BF16),
        (ATT_WIDTH, BF16), (2 * LANES, BF16), (2 * LANES, BF16), (D_MODEL, BF16), (D_MODEL, BF16),
    ]
    return pl.pallas_call(
        _inproj_kernel,
        grid=(m // tm,),
        in_specs=[row(D_MODEL), full(g1),
                  pl.BlockSpec(w.shape, lambda i: (0, 0), pipeline_mode=pl.Buffered(1)),
                  full(lbl), full(og), full(qg), full(kg), full(bdq), full(bdk)],
        out_specs=[row(c) for c, _ in outs],
        out_shape=[jax.ShapeDtypeStruct((m, c), dt) for c, dt in outs],
        scratch_shapes=[pltpu.VMEM(w.shape, BF16)],
        compiler_params=pltpu.CompilerParams(dimension_semantics=("arbitrary",), vmem_limit_bytes=VMEM_LIMIT),
        name="inproj",
    )(x2, g1, w, lbl, og, qg, kg, bdq, bdk)


def _hgrn_head(q, k, v, g, sg, st, level):
    c = HG_CHUNK
    n = HG_TILE // c
    qs = [q[i * c:(i + 1) * c].astype(BF16) for i in range(n)]
    ks = [k[i * c:(i + 1) * c].astype(BF16) for i in range(n)]
    gs = [g[i * c:(i + 1) * c] for i in range(n)]
    mid = [gc[c // 2 - 1:c // 2] for gc in gs]
    end = [gc[c - 1:c] for gc in gs]
    p2 = lambda x: jnp.exp2(x).astype(BF16)
    cat = lambda parts: jnp.concatenate(parts, axis=0)
    zero = jnp.zeros((c, HG_DK), BF16)

    q1 = [qs[i] * p2(gs[i] - mid[i]) for i in range(n)]
    k1 = [ks[i] * p2(mid[i] - gs[i]) for i in range(n)]
    s1 = _dot_nt(cat(q1), cat(k1))
    q_moves = {(1, 0): mid[1] - end[0], (3, 2): mid[3] - end[2], (2, 1): mid[2] - end[1], (3, 1): mid[3] - end[1]}
    q_moves.update({(i, "s"): mid[i] for i in range(n)})
    k_moves = {(0, 0): end[0] - mid[0], (2, 2): end[2] - mid[2], (0, 1): end[1] - mid[0], (1, 1): end[1] - mid[1]}
    k_moves.update({(i, n - 1): end[n - 1] - mid[i] for i in range(n)})
    names = [("q",) + key for key in q_moves] + [("k",) + key for key in k_moves]
    stacked = jnp.exp2(jnp.concatenate(list(q_moves.values()) + list(k_moves.values()), axis=0))
    decay = {name: stacked[r:r + 1].astype(BF16) for r, name in enumerate(names)}
    q_to = lambda i, j: q1[i] * decay[("q", i, j)]
    k_to = lambda i, j: k1[i] * decay[("k", i, j)]
    s2 = _dot_nt(cat([zero, q_to(1, 0), zero, q_to(3, 2)]), cat([k_to(0, 0), zero, k_to(2, 2), zero]))
    s3 = _dot_nt(cat([zero, zero, q_to(2, 1), q_to(3, 1)]), cat([k_to(0, 1), k_to(1, 1), zero, zero]))
    scores = jnp.where(level == 1, s1, jnp.where(level == 2, s2, s3))
    o = _dot(scores.astype(BF16), v)

    k_in = cat([k_to(i, n - 1) for i in range(n)])
    q_in = cat([q_to(i, "s") for i in range(n)])
    o = o + _dot_nt(q_in, st.astype(BF16))
    st_new = st * jnp.exp2(end[3]) + _dot_tn(v, k_in)

    ms = jnp.mean(o * o, axis=-1, keepdims=True)
    return (o * lax.rsqrt(ms + EPS)) * sg, st_new


def _hgrn_prefix(i, lf_ref, tri_ref):
    lf = lf_ref[i * HG_TILE:(i + 1) * HG_TILE, :]
    lf_hi = lf.astype(BF16)
    lf_lo = (lf - lf_hi.astype(F32)).astype(BF16)
    return _dot(tri_ref[...], jnp.concatenate([lf_hi, lf_lo], axis=0))


def _hgrn_step(i, h, g_all, q_ref, k_ref, v_ref, sg_ref, level, o_ref, st_ref):
    rs = slice(i * HG_TILE, (i + 1) * HG_TILE)
    cs = slice(h * HG_DK, (h + 1) * HG_DK)
    o, st_new = _hgrn_head(q_ref[rs, cs], k_ref[rs, cs], v_ref[rs, cs], g_all[:, cs],
                           sg_ref[rs, cs].astype(F32), st_ref[h], level)
    st_ref[h] = st_new
    o_ref[rs, cs] = o.astype(BF16)


def _swa_step(i, j, first_tile, sink_ref, q_ref, kc_ref, kp_ref, vc_ref, vp_ref, bias_ref, o_ref):
    w = WINDOW
    lane = lax.broadcasted_iota(jnp.int32, (1, LANES), 1)
    lo_b = (lane < ATT_HD).astype(BF16)
    hi_b = (lane >= ATT_HD).astype(BF16)
    lo = lax.broadcasted_iota(jnp.int32, (w, LANES), 1) < ATT_HD
    row0 = lax.broadcasted_iota(jnp.int32, (SUBLANES, LANES), 0) == 0
    not_row0 = (lax.broadcasted_iota(jnp.int32, (BF16_ROWS, LANES), 0) > 0).astype(BF16)

    kvh = j // (ATT_GROUP // 2)
    cs = slice(kvh * LANES, (kvh + 1) * LANES)
    if i == 0:
        kk = jnp.concatenate([kp_ref[:, cs], kc_ref[0:w, cs]], axis=0)
        vv = jnp.concatenate([vp_ref[0:BF16_ROWS, cs] * not_row0, vp_ref[BF16_ROWS:, cs], vc_ref[0:w, cs]], axis=0)
    else:
        r0 = (i - 1) * w
        kk = kc_ref[r0:r0 + 2 * w, cs]
        vv = jnp.concatenate([vc_ref[r0:r0 + BF16_ROWS, cs] * not_row0,
                              vc_ref[r0 + BF16_ROWS:r0 + 2 * w, cs]], axis=0)
    qc = q_ref[i * w:(i + 1) * w, j * LANES:(j + 1) * LANES]
    qs = jnp.concatenate([qc * lo_b, qc * hi_b], axis=0)
    st = _dot_nt(kk, qs) + bias_ref[j]
    if i == 0:
        st = jnp.concatenate([st[:w] + jnp.where(first_tile, NEG, 0.0), st[w:]], axis=0)
    sink = jnp.concatenate([jnp.broadcast_to(sink_ref[2 * j + n] * LOG2E, (SUBLANES, LANES)) for n in range(2)],
                           axis=1)
    st = jnp.concatenate([jnp.where(jnp.concatenate([row0, row0], axis=1), sink, st[:SUBLANES]), st[SUBLANES:]],
                         axis=0)
    m = jnp.max(st, axis=0, keepdims=True)
    e = jnp.exp2(st - m)
    r = 1.0 / jnp.sum(e, axis=0, keepdims=True)
    o2 = _dot_tn(e.astype(BF16) * r.astype(BF16), vv)
    o_ref[i * w:(i + 1) * w, j * LANES:(j + 1) * LANES] = jnp.where(lo, o2[:w], o2[w:]).astype(BF16)


def _seq_kernel(sink_ref, hq_ref, hk_ref, hlf_ref, hv_ref, hg_ref, tri_ref, lvl_ref,
                aq_ref, kc_ref, kp_ref, vc_ref, vp_ref, bias_ref, oa_ref, ob_ref, st_ref, *, nsub):
    first_tile = pl.program_id(1) == 0

    @pl.when(first_tile)
    def _():
        st_ref[...] = jnp.zeros_like(st_ref)

    level = lvl_ref[...]
    assert HG_HEADS == ATT_HEADS // 2 and HG_TILE == WINDOW
    for i in range(nsub):
        g_all = _hgrn_prefix(i, hlf_ref, tri_ref)
        for h in range(HG_HEADS):
            _hgrn_step(i, h, g_all, hq_ref, hk_ref, hv_ref, hg_ref, level, oa_ref, st_ref)
            _swa_step(i, h, first_tile, sink_ref, aq_ref, kc_ref, kp_ref, vc_ref, vp_ref, bias_ref, ob_ref)


def _hgrn_consts():
    t = HG_TILE
    row = np.arange(t)[:, None]
    col = np.arange(t)[None, :]
    tri = (col <= row).astype(np.float32)
    ci, cj = row // HG_CHUNK, col // HG_CHUNK
    level = np.where((ci == cj) & (col <= row), 1, np.where((ci == cj + 1) & (cj % 2 == 0), 2, 0))
    return jnp.asarray(np.concatenate([tri, tri], axis=1), dtype=BF16), jnp.asarray(level, dtype=jnp.int32)


def _swa_bias():
    w = WINDOW
    i = np.arange(w)[:, None]
    j = np.arange(2 * w)[None, :]
    delta = (i + w - j).astype(np.float32)
    band = (delta >= 0) & (delta < w)
    assert not band[:, 0].any()
    slopes = 2.0 ** (-8.0 * np.arange(1, ATT_HEADS + 1, dtype=np.float32) / ATT_HEADS)
    per_head = np.where(band[None], -slopes[:, None, None] * delta[None] * np.float32(LOG2E), np.float32(NEG))
    pairs = per_head.astype(np.float32).reshape(ATT_HEADS // 2, 2 * w, 2 * w)
    return jnp.asarray(np.ascontiguousarray(pairs.transpose(0, 2, 1)))


def _seq(sinks, hq, hk, hlf, hv, hg, aq, ak, av, ts):
    b, s, _ = hq.shape
    tri, level = _hgrn_consts()
    bias = _swa_bias()
    nprev = ts // WINDOW
    wide = pl.BlockSpec((None, ts, HG_WIDTH), lambda bi, si, *_: (bi, si, 0))
    cur = pl.BlockSpec((None, ts, 2 * LANES), lambda bi, si, *_: (bi, si, 0))
    prev = pl.BlockSpec((None, WINDOW, 2 * LANES), lambda bi, si, *_: (bi, jnp.maximum(si * nprev - 1, 0), 0))
    full = lambda a: pl.BlockSpec(a.shape, lambda bi, si, *_: (0,) * a.ndim)
    return pl.pallas_call(
        functools.partial(_seq_kernel, nsub=ts // HG_TILE),
        grid_spec=pltpu.PrefetchScalarGridSpec(
            num_scalar_prefetch=1,
            grid=(b, s // ts),
            in_specs=[wide, wide, wide, wide, wide, full(tri), full(level),
                      wide, cur, prev, cur, prev, full(bias)],
            out_specs=[wide, wide],
            scratch_shapes=[pltpu.VMEM((HG_HEADS, HG_DV, HG_DK), F32)],
        ),
        out_shape=[jax.ShapeDtypeStruct((b, s, HG_WIDTH), BF16), jax.ShapeDtypeStruct((b, s, ATT_WIDTH), BF16)],
        compiler_params=pltpu.CompilerParams(dimension_semantics=("arbitrary", "arbitrary"),
                                             vmem_limit_bytes=VMEM_LIMIT),
        name="seq",
    )(sinks, hq, hk, hlf, hv, hg, tri, level, aq, ak, ak, av, av, bias)


def _post_kernel(x_ref, oa_ref, ob_ref, ga_ref, gb_ref, wa_ref, wb_ref, wo_ref, g2_ref,
                 wu_ref, cw_ref, cb_ref, wd_ref, o_ref, x1_ref, h2_ref, gs_ref, carry_ref, act_ref, *, tm):
    pad = SUBLANES

    @pl.when(pl.program_id(1) == 0)
    def _():
        carry_ref[...] = jnp.zeros_like(carry_ref)

    mixed = (ga_ref[...].astype(F32) * _dot(oa_ref[...], wa_ref[...])
             + gb_ref[...].astype(F32) * _dot(ob_ref[...], wb_ref[...]))
    x1_ref[...] = x_ref[...] + _dot(mixed.astype(BF16), wo_ref[...])
    h2_ref[...] = _rms(x1_ref[...], g2_ref[...]).astype(BF16)

    h2 = h2_ref[...]
    for c0 in range(0, D_FF, FF_CHUNK):
        cw = min(FF_CHUNK, D_FF - c0)
        cs = slice(c0, c0 + cw)
        gate = _dot(h2, wu_ref[:, cs])
        val = _dot(h2, wu_ref[:, D_FF + c0:D_FF + c0 + cw])
        gs_ref[0:pad, :cw] = carry_ref[:, cs]
        gs_ref[pad:pad + tm, :cw] = gate
        carry_ref[:, cs] = gate[tm - pad:tm, :]
        conv = cb_ref[:, cs] + cw_ref[2:3, cs] * gate
        conv = conv + cw_ref[1:2, cs] * gs_ref[pad - 1:pad - 1 + tm, :cw]
        conv = conv + cw_ref[0:1, cs] * gs_ref[pad - 2:pad - 2 + tm, :cw]
        gelu = (0.5 * conv) * (1.0 + lax.erf(conv * (0.5 ** 0.5)))
        act_ref[:, cs] = (gelu * val).astype(BF16)
    o_ref[...] = x1_ref[...] + _dot(act_ref[...], wd_ref[...])


def _post(x, oa, ob, ga, gb, wa, wb, wo, g2, wu, cw, cb, wd, tm):
    b, s, _ = x.shape
    row = lambda c: pl.BlockSpec((None, tm, c), lambda bi, si: (bi, si, 0))
    full = lambda a: pl.BlockSpec(a.shape, lambda bi, si: (0,) * a.ndim, pipeline_mode=pl.Buffered(1))
    return pl.pallas_call(
        functools.partial(_post_kernel, tm=tm),
        grid=(b, s // tm),
        in_specs=[row(D_MODEL), row(HG_WIDTH), row(ATT_WIDTH), row(D_MODEL), row(D_MODEL),
                  full(wa), full(wb), full(wo), full(g2), full(wu), full(cw), full(cb), full(wd)],
        out_specs=row(D_MODEL),
        out_shape=jax.ShapeDtypeStruct((b, s, D_MODEL), F32),
        scratch_shapes=[pltpu.VMEM((tm, D_MODEL), F32),
                        pltpu.VMEM((tm, D_MODEL), BF16),
                        pltpu.VMEM((tm + SUBLANES, FF_CHUNK), F32),
                        pltpu.VMEM((SUBLANES, D_FF), F32),
                        pltpu.VMEM((tm, D_FF), BF16)],
        compiler_params=pltpu.CompilerParams(dimension_semantics=("arbitrary", "arbitrary"),
                                             vmem_limit_bytes=VMEM_LIMIT),
        name="post",
    )(x, oa, ob, ga, gb, wa, wb, wo, g2, wu, cw, cb, wd)


def _block_diag_ones(n, blk):
    idx = np.arange(n) // blk
    return jnp.asarray((idx[:, None] == idx[None, :]).astype(np.float32), dtype=BF16)


def kernel(x, norm1_g, w_in, hgrn_lb_logits, hgrn_out_g, q_norm_g, k_norm_g, attn_sinks,
           w_branch_a, w_branch_b, w_out, norm2_g, w_up, conv_w, conv_b, w_down):
    b, s, d = x.shape
    assert d == D_MODEL and w_in.shape == (1, D_MODEL, IN_COLS)
    m = b * s
    tm, ts = _tiles(s)

    og = jnp.tile(hgrn_out_g[0], HG_HEADS).reshape(1, HG_WIDTH)
    qg = jnp.tile(q_norm_g[0], ATT_HEADS).reshape(1, ATT_WIDTH)
    kg = jnp.tile(k_norm_g[0], ATT_KV_HEADS).reshape(1, ATT_KV_WIDTH)
    hq, hk, hlf, hv, hg, aq, ak, av, ga, gb = _inproj(
        x.reshape(m, d), norm1_g[0].reshape(1, d), w_in[0], hgrn_lb_logits, og, qg, kg,
        _block_diag_ones(2 * LANES, ATT_HD), _block_diag_ones(ATT_KV_WIDTH, ATT_HD), tm)

    r3 = lambda a: a.reshape(b, s, a.shape[-1])
    o_a, o_b = _seq(attn_sinks[0], r3(hq), r3(hk), r3(hlf), r3(hv), r3(hg), r3(aq), r3(ak), r3(av), ts)

    return _post(x, o_a, o_b, r3(ga), r3(gb),
                 w_branch_a[0].astype(BF16), w_branch_b[0].astype(BF16), w_out[0].astype(BF16),
                 norm2_g[0].reshape(1, d), w_up[0].astype(BF16),
                 conv_w[0], conv_b[0].reshape(1, D_FF), w_down[0].astype(BF16), tm)
```
